```python
import jax
import jax.numpy as jnp
from jax import lax
import numpy as np

D_MODEL = 1024
BATCH = 8
SEQ = 2048
DEPTH = 1

LRU_HEADS = 8
LRU_WIDTH = D_MODEL // 2
LRU_HEAD_DIM = LRU_WIDTH // LRU_HEADS
LRU_CONV = 4
LRU_C = 8.0
ATTN_HEADS = 8
ATTN_WIDTH = D_MODEL // 2
HEAD_DIM = ATTN_WIDTH // ATTN_HEADS
MIX_WIDTH = LRU_WIDTH + ATTN_WIDTH
IN_WIDTH = 2 * LRU_WIDTH + 3 * ATTN_WIDTH
MOBA_BLOCK = 256
MOBA_TOPK = 3
Q_CHUNK = 32
D_FF = 2816
FFN_CONV = 3
EPS = 1e-6

kernel_name = 'hybrid_rglru_moba_convffn'


def rms_norm(x, g):
    xf = x.astype(jnp.float32)
    y = xf * lax.rsqrt(jnp.mean(xf * xf, axis=-1, keepdims=True) + EPS)
    return (y * g.astype(jnp.float32)).astype(x.dtype)


def causal_depthwise_conv(x, w, b):
    width, ch = w.shape
    y = lax.conv_general_dilated(
        x, w[:, None, :].astype(x.dtype), window_strides=(1,),
        padding=[(width - 1, 0)], dimension_numbers=('NWC', 'WIO', 'NWC'),
        feature_group_count=ch)
    return y + b.astype(x.dtype)


def rg_lru(xr, w_a, b_a, w_i, b_i, lam):
    bsz, seq, _ = xr.shape
    xf = xr.astype(jnp.float32)
    xh = xf.reshape(bsz, seq, LRU_HEADS, LRU_HEAD_DIM)
    r = jax.nn.sigmoid(jnp.einsum('bshi,hij->bshj', xh, w_a.astype(jnp.float32)).reshape(bsz, seq, LRU_WIDTH) + b_a.astype(jnp.float32))
    i = jax.nn.sigmoid(jnp.einsum('bshi,hij->bshj', xh, w_i.astype(jnp.float32)).reshape(bsz, seq, LRU_WIDTH) + b_i.astype(jnp.float32))
    log_a = -LRU_C * r * jax.nn.softplus(-lam.astype(jnp.float32))
    a = jnp.exp(log_a)
    u = jnp.sqrt(-jnp.expm1(2.0 * log_a)) * (i * xf)

    def combine(left, right):
        a_l, h_l = left
        a_r, h_r = right
        return a_l * a_r, a_r * h_l + h_r

    _, h = lax.associative_scan(combine, (a, u), axis=1)
    return h


def moba_attention(q, k, v):
    bsz, seq, heads, dh = q.shape
    n_blocks = -(-seq // MOBA_BLOCK)
    s_pad = n_blocks * MOBA_BLOCK
    k_sel = max(1, min(MOBA_TOPK, n_blocks - 1))
    scale = dh ** -0.5
    neg = jnp.finfo(jnp.float32).min
    slopes = jnp.exp2(-8.0 * jnp.arange(1, heads + 1, dtype=jnp.float32) / heads)
    qh = q.astype(jnp.float32).transpose(0, 2, 1, 3)
    pad = ((0, 0), (0, 0), (0, s_pad - seq), (0, 0))
    kb = jnp.pad(k.astype(jnp.float32).transpose(0, 2, 1, 3), pad).reshape(bsz, heads, n_blocks, MOBA_BLOCK, dh)
    vb = jnp.pad(v.astype(jnp.float32).transpose(0, 2, 1, 3), pad).reshape(bsz, heads, n_blocks, MOBA_BLOCK, dh)
    k_mean = kb.mean(axis=3)
    q_blk = jnp.arange(seq) // MOBA_BLOCK
    fully_past = jnp.arange(n_blocks)[None, :] < q_blk[:, None]
    gate = jnp.where(fully_past, jnp.einsum('bhsd,bhnd->bhsn', qh, k_mean), neg)
    _, sel = lax.top_k(gate, k_sel)
    sel_ok = sel < q_blk[:, None]
    gather_blocks = jax.vmap(jax.vmap(lambda blocks, ids: blocks[ids]))
    offs = jnp.arange(MOBA_BLOCK)

    def query_chunk(c):
        start = c * Q_CHUNK
        q_c = lax.dynamic_slice_in_dim(qh, start, Q_CHUNK, axis=2)
        sel_c = lax.dynamic_slice_in_dim(sel, start, Q_CHUNK, axis=2)
        ok_c = lax.dynamic_slice_in_dim(sel_ok, start, Q_CHUNK, axis=2)
        q_pos = start + jnp.arange(Q_CHUNK)
        k_g = gather_blocks(kb, sel_c)
        v_g = gather_blocks(vb, sel_c)
        dist_g = (q_pos[:, None, None] - (sel_c[..., None] * MOBA_BLOCK + offs)).astype(jnp.float32)
        s_g = jnp.einsum('bhqd,bhqnkd->bhqnk', q_c, k_g) * scale - slopes[:, None, None, None] * dist_g
        s_g = jnp.where(ok_c[..., None], s_g, neg)
        own = start // MOBA_BLOCK
        k_o = lax.dynamic_index_in_dim(kb, own, axis=2, keepdims=False)
        v_o = lax.dynamic_index_in_dim(vb, own, axis=2, keepdims=False)
        dist_o = q_pos[:, None] - (own * MOBA_BLOCK + offs)[None, :]
        s_o = jnp.einsum('bhqd,bhkd->bhqk', q_c, k_o) * scale - slopes[:, None, None] * dist_o.astype(jnp.float32)
        s_o = jnp.where(dist_o >= 0, s_o, neg)
        logits = jnp.concatenate([s_g.reshape(bsz, heads, Q_CHUNK, k_sel * MOBA_BLOCK), s_o], axis=-1)
        p = jax.nn.softmax(logits, axis=-1)
        p_g = p[..., :k_sel * MOBA_BLOCK].reshape(bsz, heads, Q_CHUNK, k_sel, MOBA_BLOCK)
        p_o = p[..., k_sel * MOBA_BLOCK:]
        return jnp.einsum('bhqnk,bhqnkd->bhqd', p_g, v_g) + jnp.einsum('bhqk,bhkd->bhqd', p_o, v_o)

    out = lax.map(query_chunk, jnp.arange(seq // Q_CHUNK))
    return out.transpose(1, 0, 3, 2, 4).reshape(bsz, seq, heads * dh)


def setup_inputs(seed: int = 0) -> dict:
    key = jax.random.key(seed)
    ks = jax.random.split(key, 24)
    f32 = jnp.float32

    def nrm(k, shape, scale):
        return jax.random.normal(k, shape, f32) * scale

    def gain(k, n):
        return 1.0 + 0.05 * jax.random.normal(k, (DEPTH, n), f32)

    u = jax.random.uniform(ks[11], (DEPTH, LRU_WIDTH), f32, 0.9, 0.999)
    a_base = u ** (1.0 / LRU_C)
    return {
        'x': nrm(ks[0], (BATCH, SEQ, D_MODEL), 1.0),
        'mix_norm_g': gain(ks[1], D_MODEL),
        'w_in': nrm(ks[2], (DEPTH, D_MODEL, IN_WIDTH), D_MODEL ** -0.5),
        'lru_conv_w': nrm(ks[3], (DEPTH, LRU_CONV, LRU_WIDTH), LRU_CONV ** -0.5),
        'lru_conv_b': nrm(ks[4], (DEPTH, LRU_WIDTH), 0.02),
        'lru_w_a': nrm(ks[5], (DEPTH, LRU_HEADS, LRU_HEAD_DIM, LRU_HEAD_DIM), LRU_HEAD_DIM ** -0.5),
        'lru_b_a': nrm(ks[6], (DEPTH, LRU_WIDTH), 0.02),
        'lru_w_i': nrm(ks[7], (DEPTH, LRU_HEADS, LRU_HEAD_DIM, LRU_HEAD_DIM), LRU_HEAD_DIM ** -0.5),
        'lru_b_i': nrm(ks[8], (DEPTH, LRU_WIDTH), 0.02),
        'lru_lambda': jnp.log(a_base) - jnp.log1p(-a_base),
        'q_norm_g': gain(ks[9], HEAD_DIM),
        'k_norm_g': gain(ks[10], HEAD_DIM),
        'lru_out_g': gain(ks[12], LRU_WIDTH),
        'attn_out_g': gain(ks[13], ATTN_WIDTH),
        'w_out': nrm(ks[14], (DEPTH, MIX_WIDTH, D_MODEL), MIX_WIDTH ** -0.5),
        'ffn_norm_g': gain(ks[15], D_MODEL),
        'w_up': nrm(ks[16], (DEPTH, D_MODEL, 2 * D_FF), D_MODEL ** -0.5),
        'ffn_conv_w': nrm(ks[17], (DEPTH, FFN_CONV, 2 * D_FF), FFN_CONV ** -0.5),
        'ffn_conv_b': nrm(ks[18], (DEPTH, 2 * D_FF), 0.02),
        'w_down': nrm(ks[19], (DEPTH, D_FF, D_MODEL), D_FF ** -0.5),
    }


def reference(x, mix_norm_g, w_in, lru_conv_w, lru_conv_b, lru_w_a, lru_b_a, lru_w_i, lru_b_i, lru_lambda, q_norm_g, k_norm_g, lru_out_g, attn_out_g, w_out, ffn_norm_g, w_up, ffn_conv_w, ffn_conv_b, w_down):
    bsz, seq, _ = x.shape
    splits = [LRU_WIDTH, 2 * LRU_WIDTH, 2 * LRU_WIDTH + ATTN_WIDTH, 2 * LRU_WIDTH + 2 * ATTN_WIDTH]
    for layer in range(DEPTH):
        h = rms_norm(x, mix_norm_g[layer])
        proj = h @ w_in[layer]
        xr, yg, q, k, v = jnp.split(proj, splits, axis=-1)
        xr = causal_depthwise_conv(xr, lru_conv_w[layer], lru_conv_b[layer])
        y_lru = rg_lru(xr, lru_w_a[layer], lru_b_a[layer], lru_w_i[layer], lru_b_i[layer], lru_lambda[layer]) * jax.nn.gelu(yg.astype(jnp.float32))
        q = rms_norm(q.reshape(bsz, seq, ATTN_HEADS, HEAD_DIM), q_norm_g[layer])
        k = rms_norm(k.reshape(bsz, seq, ATTN_HEADS, HEAD_DIM), k_norm_g[layer])
        v = v.reshape(bsz, seq, ATTN_HEADS, HEAD_DIM)
        y_attn = moba_attention(q, k, v)
        mixed = jnp.concatenate([rms_norm(y_lru, lru_out_g[layer]), rms_norm(y_attn, attn_out_g[layer])], axis=-1).astype(x.dtype)
        x = x + mixed @ w_out[layer]
        hf = rms_norm(x, ffn_norm_g[layer])
        up = causal_depthwise_conv(hf @ w_up[layer], ffn_conv_w[layer], ffn_conv_b[layer])
        gate, val = jnp.split(up, 2, axis=-1)
        x = x + (jax.nn.gelu(gate) * val) @ w_down[layer]
    return x
```

```python
import functools

import jax
import jax.numpy as jnp
from jax import lax
from jax.experimental import pallas as pl
from jax.experimental.pallas import tpu as pltpu

F32 = jnp.float32
BF16 = jnp.bfloat16

D_MODEL = 1024
LRU_WIDTH = 512
LRU_HEADS = 8
LRU_CONV = 4
LRU_C = 8.0
ATTN_WIDTH = 512
ATTN_HEADS = 8
HEAD_DIM = 64
IN_WIDTH = 2 * LRU_WIDTH + 3 * ATTN_WIDTH
MOBA_BLOCK = 256
MOBA_TOPK = 3
D_FF = 2816
FFN_CONV = 3
EPS = 1e-6

LANES = 128
SUBLANES = 8
BF16_ROWS = 16
PAIR_WIDTH = 2 * HEAD_DIM
N_PAIRS = ATTN_HEADS // 2
MASK_NEG = -1e30

TM_MIX = MOBA_BLOCK
TM_FFN = 256
TF = 256
VMEM_LIMIT = 56 * 1024 * 1024


def _rms(x, g):
    return x * lax.rsqrt(jnp.mean(x * x, axis=-1, keepdims=True) + EPS) * g


def _dot(a, b):
    return jnp.dot(a, b, preferred_element_type=F32)


def _dot_nt(a, b):
    return lax.dot_general(a, b, (((1,), (1,)), ((), ())), preferred_element_type=F32)


def _mix_in_kernel(x_ref, ng_ref, win_ref, cw_ref, cb_ref, wa_ref, ba_ref, wi_ref, bi_ref,
                   lam_ref, qg_ref, kg_ref, og_ref,
                   ylru_ref, q_ref, k_ref, v_ref, km_ref,
                   xe_ref, hc_ref):
    t = pl.program_id(1)
    tm = TM_MIX

    @pl.when(t == 0)
    def _():
        xe_ref[0:SUBLANES, :] = jnp.zeros((SUBLANES, LRU_WIDTH), F32)
        hc_ref[...] = jnp.zeros((SUBLANES, LRU_WIDTH), F32)

    h = _rms(x_ref[...], ng_ref[...]).astype(BF16)

    xr = _dot(h, win_ref[:, 0:LRU_WIDTH])
    xe_ref[SUBLANES:SUBLANES + tm, :] = xr
    xc = cb_ref[...]
    for j in range(LRU_CONV):
        off = SUBLANES - (LRU_CONV - 1) + j
        xc = xc + cw_ref[j:j + 1, :] * xe_ref[off:off + tm, :]
    xe_ref[0:SUBLANES, :] = xr[tm - SUBLANES:tm, :]

    xcb = xc.astype(BF16)
    half = LRU_WIDTH // 2
    r = jnp.concatenate([_dot(xcb[:, g * half:(g + 1) * half], wa_ref[g]) for g in range(2)], axis=-1)
    i = jnp.concatenate([_dot(xcb[:, g * half:(g + 1) * half], wi_ref[g]) for g in range(2)], axis=-1)
    r = jax.nn.sigmoid(r + ba_ref[...])
    i = jax.nn.sigmoid(i + bi_ref[...])
    nlam = -lam_ref[...]
    softplus = jnp.maximum(nlam, 0.0) + jnp.log1p(jnp.exp(-jnp.abs(nlam)))
    log_a = (-LRU_C * r) * softplus
    a = jnp.exp(log_a)
    u = jnp.sqrt(1.0 - jnp.exp(2.0 * log_a)) * (i * xc)

    row = lax.broadcasted_iota(jnp.int32, (SUBLANES, LRU_WIDTH), 0)
    carry = hc_ref[SUBLANES - 1:SUBLANES, :]
    slabs = []
    for g in range(tm // SUBLANES):
        ag = a[g * SUBLANES:(g + 1) * SUBLANES, :]
        hg = u[g * SUBLANES:(g + 1) * SUBLANES, :]
        for d in (1, 2, 4):
            keep = row >= d
            a_sh = jnp.where(keep, pltpu.roll(ag, d, 0), 1.0)
            h_sh = jnp.where(keep, pltpu.roll(hg, d, 0), 0.0)
            hg = hg + ag * h_sh
            ag = ag * a_sh
        hg = hg + ag * carry
        carry = hg[SUBLANES - 1:SUBLANES, :]
        slabs.append(hg)
    hc_ref[...] = slabs[-1]
    hs = jnp.concatenate(slabs, axis=0)

    yg = _dot(h, win_ref[:, LRU_WIDTH:2 * LRU_WIDTH])
    ylru_ref[...] = _rms(hs * jax.nn.gelu(yg), og_ref[...]).astype(BF16)

    ri = lax.broadcasted_iota(jnp.int32, (LANES, LANES), 0) >> 6
    ci = lax.broadcasted_iota(jnp.int32, (LANES, LANES), 1) >> 6
    head_ones = jnp.where(ri == ci, 1.0, 0.0).astype(BF16)

    def head_norm(z, g):
        sq = (z * z).astype(BF16)
        ssq = jnp.concatenate(
            [_dot(sq[:, c * LANES:(c + 1) * LANES], head_ones) for c in range(ATTN_WIDTH // LANES)], axis=-1)
        return z * lax.rsqrt(ssq * (1.0 / HEAD_DIM) + EPS) * g

    base = 2 * LRU_WIDTH
    q = _dot(h, win_ref[:, base:base + ATTN_WIDTH])
    q_ref[...] = (head_norm(q, qg_ref[...]) * (HEAD_DIM ** -0.5)).astype(BF16)
    k = _dot(h, win_ref[:, base + ATTN_WIDTH:base + 2 * ATTN_WIDTH])
    kn = head_norm(k, kg_ref[...])
    k_ref[...] = kn.astype(BF16)
    km_ref[...] = jnp.mean(kn, axis=0, keepdims=True)
    v = _dot(h, win_ref[:, base + 2 * ATTN_WIDTH:base + 3 * ATTN_WIDTH])
    v_ref[...] = v.astype(BF16)


def _mix_in(x, ng, win, cw, cb, wa, ba, wi, bi, lam, qg, kg, og):
    bsz, seq, _ = x.shape
    nt = seq // TM_MIX
    full = lambda shape: pl.BlockSpec(shape, lambda b, t: (0,) * len(shape))
    tile = lambda w: pl.BlockSpec((None, TM_MIX, w), lambda b, t: (b, t, 0))
    return pl.pallas_call(
        _mix_in_kernel,
        grid=(bsz, nt),
        in_specs=[
            tile(D_MODEL),
            full((1, D_MODEL)),
            full((D_MODEL, IN_WIDTH)),
            full((LRU_CONV, LRU_WIDTH)),
            full((1, LRU_WIDTH)),
            full((2, LRU_WIDTH // 2, LRU_WIDTH // 2)),
            full((1, LRU_WIDTH)),
            full((2, LRU_WIDTH // 2, LRU_WIDTH // 2)),
            full((1, LRU_WIDTH)),
            full((1, LRU_WIDTH)),
            full((1, ATTN_WIDTH)),
            full((1, ATTN_WIDTH)),
            full((1, LRU_WIDTH)),
        ],
        out_specs=[
            tile(LRU_WIDTH), tile(ATTN_WIDTH), tile(ATTN_WIDTH), tile(ATTN_WIDTH),
            pl.BlockSpec((None, None, 1, ATTN_WIDTH), lambda b, t: (b, t, 0, 0)),
        ],
        out_shape=[
            jax.ShapeDtypeStruct((bsz, seq, LRU_WIDTH), BF16),
            jax.ShapeDtypeStruct((bsz, seq, ATTN_WIDTH), BF16),
            jax.ShapeDtypeStruct((bsz, seq, ATTN_WIDTH), BF16),
            jax.ShapeDtypeStruct((bsz, seq, ATTN_WIDTH), BF16),
            jax.ShapeDtypeStruct((bsz, nt, 1, ATTN_WIDTH), F32),
        ],
        scratch_shapes=[
            pltpu.VMEM((SUBLANES + TM_MIX, LRU_WIDTH), F32),
            pltpu.VMEM((SUBLANES, LRU_WIDTH), F32),
        ],
        compiler_params=pltpu.CompilerParams(
            dimension_semantics=("arbitrary", "arbitrary"), vmem_limit_bytes=VMEM_LIMIT),
        name="mix_in",
    )(x, ng, win, cw, cb, wa, ba, wi, bi, lam, qg, kg, og)


def _moba_kernel(q_ref, k_ref, v_ref, km_ref, o_ref, vt_ref, selb_ref):
    pair = pl.program_id(1)
    qi = pl.program_id(2)
    blk = MOBA_BLOCK
    nb = km_ref.shape[0]

    @pl.when(qi == 0)
    def _():
        vt_ref[...] = v_ref[...].astype(F32).T.astype(BF16)

    q = q_ref[...]
    kmb = km_ref[...].astype(BF16)
    lane_head = lax.broadcasted_iota(jnp.int32, (blk, PAIR_WIDTH), 1) >> 6
    dist = (lax.broadcasted_iota(jnp.int32, (blk, blk), 1)
            - lax.broadcasted_iota(jnp.int32, (blk, blk), 0))
    dist_f = dist.astype(F32)
    blk_row = lax.broadcasted_iota(jnp.int32, (nb, blk), 0)
    diag_start = pl.multiple_of(qi * blk, blk)

    head_out = []
    for hh in range(2):
        head = 2 * pair + hh
        slope = lax.bitcast_convert_type(
            jnp.broadcast_to((126 - head) << 23, (1, blk)).astype(jnp.int32), F32)
        qh = jnp.where(lane_head == hh, q, jnp.zeros_like(q))

        gate = _dot_nt(kmb, qh)
        valid = blk_row < qi
        g = jnp.where(valid, gate, jnp.finfo(F32).min)
        rank = jnp.zeros((nb, blk), F32)
        for m in range(nb):
            gm = g[m:m + 1, :]
            beats = jnp.where(gm > g, 1.0, jnp.where((gm == g) & (blk_row > m), 1.0, 0.0))
            rank = rank + jnp.where(m < qi, beats, 0.0)
        sel = valid & (rank < float(MOBA_TOPK))
        selb_ref[hh * nb:(hh + 1) * nb, :] = jnp.where(sel, 0.0, MASK_NEG)

        kd = k_ref[pl.ds(diag_start, blk), :]
        s = _dot_nt(kd, qh)
        s = jnp.where(dist >= 0, s - slope * dist_f, MASK_NEG)
        m0 = jnp.max(s, axis=0, keepdims=True)
        p = jnp.exp(s - m0)
        l0 = jnp.sum(p, axis=0, keepdims=True)
        acc0 = _dot(vt_ref[:, pl.ds(diag_start, blk)], p.astype(BF16))

        def past(j, carry):
            m_run, l_run, acc = carry
            start = pl.multiple_of(j * blk, blk)
            kj = k_ref[pl.ds(start, blk), :]
            sj = _dot_nt(kj, qh)
            gap = ((qi - j) * blk).astype(F32)
            sj = sj - slope * (dist_f + gap) + selb_ref[pl.ds(hh * nb + j, 1), :]
            m_new = jnp.maximum(m_run, jnp.max(sj, axis=0, keepdims=True))
            alpha = jnp.exp(m_run - m_new)
            pj = jnp.exp(sj - m_new)
            l_new = alpha * l_run + jnp.sum(pj, axis=0, keepdims=True)
            acc_new = alpha * acc + _dot(vt_ref[:, pl.ds(start, blk)], pj.astype(BF16))
            return m_new, l_new, acc_new

        _, l_fin, acc_fin = lax.fori_loop(0, qi, past, (m0, l0, acc0))
        head_out.append(acc_fin / l_fin)

    dim_head = lax.broadcasted_iota(jnp.int32, (PAIR_WIDTH, blk), 0) >> 6
    out_t = jnp.where(dim_head == 0, head_out[0], head_out[1])
    o_ref[...] = out_t.T


def _moba(q, k, v, kmean):
    bsz, seq, _ = q.shape
    nb = seq // MOBA_BLOCK
    return pl.pallas_call(
        _moba_kernel,
        grid=(bsz, N_PAIRS, nb),
        in_specs=[
            pl.BlockSpec((None, MOBA_BLOCK, PAIR_WIDTH), lambda b, p, i: (b, i, p)),
            pl.BlockSpec((None, seq, PAIR_WIDTH), lambda b, p, i: (b, 0, p)),
            pl.BlockSpec((None, seq, PAIR_WIDTH), lambda b, p, i: (b, 0, p)),
            pl.BlockSpec((None, nb, PAIR_WIDTH), lambda b, p, i: (b, 0, p)),
        ],
        out_specs=pl.BlockSpec((None, MOBA_BLOCK, PAIR_WIDTH), lambda b, p, i: (b, i, p)),
        out_shape=jax.ShapeDtypeStruct((bsz, seq, ATTN_WIDTH), F32),
        scratch_shapes=[
            pltpu.VMEM((PAIR_WIDTH, seq), BF16),
            pltpu.VMEM((2 * nb, MOBA_BLOCK), F32),
        ],
        compiler_params=pltpu.CompilerParams(
            dimension_semantics=("arbitrary", "arbitrary", "arbitrary"), vmem_limit_bytes=VMEM_LIMIT),
        name="moba",
    )(q, k, v, kmean)


def _ffn_kernel(x_ref, yl_ref, ya_ref, ag_ref, wout_ref, fg_ref, wup_ref, cw_ref, cb_ref, wdn_ref,
                o_ref, hf_ref, up_ref):
    t = pl.program_id(1)
    tm = TM_FFN
    halo = BF16_ROWS

    @pl.when(t == 0)
    def _():
        hf_ref[0:halo, :] = jnp.zeros((halo, D_MODEL), BF16)

    an = _rms(ya_ref[...], ag_ref[...]).astype(BF16)
    x1 = (x_ref[...] + _dot(yl_ref[...], wout_ref[0:LRU_WIDTH, :])
          + _dot(an, wout_ref[LRU_WIDTH:LRU_WIDTH + ATTN_WIDTH, :]))
    hf = _rms(x1, fg_ref[...]).astype(BF16)
    hf_ref[halo:halo + tm, :] = hf

    def chunk(c, acc):
        off_g = pl.multiple_of(c * TF, TF)
        off_v = pl.multiple_of(D_FF + c * TF, LANES)
        hfe = hf_ref[...]
        up_ref[:, 0:TF] = _dot(hfe, wup_ref[:, pl.ds(off_g, TF)])
        up_ref[:, TF:2 * TF] = _dot(hfe, wup_ref[:, pl.ds(off_v, TF)])

        def conv(col0, off):
            w = cw_ref[:, pl.ds(off, TF)]
            y = cb_ref[:, pl.ds(off, TF)]
            for j in range(FFN_CONV):
                r0 = halo - (FFN_CONV - 1) + j
                y = y + w[j:j + 1, :] * up_ref[r0:r0 + tm, col0:col0 + TF]
            return y

        act = (jax.nn.gelu(conv(0, off_g)) * conv(TF, off_v)).astype(BF16)
        return acc + _dot(act, wdn_ref[pl.ds(off_g, TF), :])

    o_ref[...] = lax.fori_loop(0, D_FF // TF, chunk, x1)
    hf_ref[0:halo, :] = hf[tm - halo:tm, :]


def _ffn(x, ylru, yattn, ag, wout, fg, wup, cw, cb, wdn):
    bsz, seq, _ = x.shape
    full = lambda shape: pl.BlockSpec(shape, lambda b, t: (0,) * len(shape))
    tile = lambda w: pl.BlockSpec((None, TM_FFN, w), lambda b, t: (b, t, 0))
    return pl.pallas_call(
        _ffn_kernel,
        grid=(bsz, seq // TM_FFN),
        in_specs=[
            tile(D_MODEL), tile(LRU_WIDTH), tile(ATTN_WIDTH),
            full((1, ATTN_WIDTH)),
            full((D_MODEL, D_MODEL)),
            full((1, D_MODEL)),
            full((D_MODEL, 2 * D_FF)),
            full((FFN_CONV, 2 * D_FF)),
            full((1, 2 * D_FF)),
            full((D_FF, D_MODEL)),
        ],
        out_specs=tile(D_MODEL),
        out_shape=jax.ShapeDtypeStruct((bsz, seq, D_MODEL), F32),
        scratch_shapes=[
            pltpu.VMEM((BF16_ROWS + TM_FFN, D_MODEL), BF16),
            pltpu.VMEM((BF16_ROWS + TM_FFN, 2 * TF), F32),
        ],
        compiler_params=pltpu.CompilerParams(
            dimension_semantics=("arbitrary", "arbitrary"), vmem_limit_bytes=VMEM_LIMIT),
        name="ffn",
    )(x, ylru, yattn, ag, wout, fg, wup, cw, cb, wdn)


def _block_diag_halves(w):
    hd = w.shape[-1]
    w4 = w.reshape(2, LRU_HEADS // 2, hd, hd)
    eye = jnp.eye(LRU_HEADS // 2, dtype=w.dtype)
    return jnp.einsum('ghij,hk->ghikj', w4, eye).reshape(2, LRU_WIDTH // 2, LRU_WIDTH // 2)


def kernel(x, mix_norm_g, w_in, lru_conv_w, lru_conv_b, lru_w_a, lru_b_a, lru_w_i, lru_b_i, lru_lambda,
           q_norm_g, k_norm_g, lru_out_g, attn_out_g, w_out, ffn_norm_g, w_up, ffn_conv_w, ffn_conv_b, w_down):
    depth = w_in.shape[0]
    bsz, seq, _ = x.shape
    assert seq % MOBA_BLOCK == 0 and seq // MOBA_BLOCK > MOBA_TOPK
    row = lambda p: p.reshape(1, -1)
    for layer in range(depth):
        ylru, q, k, v, kmean = _mix_in(
            x, row(mix_norm_g[layer]), w_in[layer].astype(BF16),
            lru_conv_w[layer], row(lru_conv_b[layer]),
            _block_diag_halves(lru_w_a[layer]).astype(BF16), row(lru_b_a[layer]),
            _block_diag_halves(lru_w_i[layer]).astype(BF16), row(lru_b_i[layer]),
            row(lru_lambda[layer]),
            row(jnp.tile(q_norm_g[layer], ATTN_HEADS)), row(jnp.tile(k_norm_g[layer], ATTN_HEADS)),
            row(lru_out_g[layer]))
        yattn = _moba(q, k, v, kmean.reshape(bsz, seq // MOBA_BLOCK, ATTN_WIDTH))
        x = _ffn(x, ylru, yattn, row(attn_out_g[layer]), w_out[layer].astype(BF16),
                 row(ffn_norm_g[layer]), w_up[layer].astype(BF16),
                 ffn_conv_w[layer], row(ffn_conv_b[layer]), w_down[layer].astype(BF16))
    return x
```

```python
import functools

import jax
import jax.numpy as jnp
from jax import lax
from jax.experimental import pallas as pl
from jax.experimental.pallas import tpu as pltpu

F32 = jnp.float32
BF16 = jnp.bfloat16

D_MODEL = 1024
LRU_WIDTH = 512
LRU_HEADS = 8
LRU_CONV = 4
LRU_C = 8.0
ATTN_WIDTH = 512
ATTN_HEADS = 8
HEAD_DIM = 64
IN_WIDTH = 2 * LRU_WIDTH + 3 * ATTN_WIDTH
MOBA_BLOCK = 256
MOBA_TOPK = 3
D_FF = 2816
FFN_CONV = 3
EPS = 1e-6

LANES = 128
SUBLANES = 8
BF16_ROWS = 16
PAIR_WIDTH = 2 * HEAD_DIM
N_PAIRS = ATTN_HEADS // 2
MASK_NEG = -1e30

TM_MIX = MOBA_BLOCK
TM_FFN = 256
TF = 256
VMEM_LIMIT = 56 * 1024 * 1024


def _rms(x, g):
    return x * lax.rsqrt(jnp.mean(x * x, axis=-1, keepdims=True) + EPS) * g


def _dot(a, b):
    return jnp.dot(a, b, preferred_element_type=F32)


def _dot_nt(a, b):
    return lax.dot_general(a, b, (((1,), (1,)), ((), ())), preferred_element_type=F32)


def _mix_in_kernel(x_ref, ng_ref, win_ref, cw_ref, cb_ref, wa_ref, ba_ref, wi_ref, bi_ref,
                   lam_ref, qg_ref, kg_ref, og_ref,
                   ylru_ref, q_ref, k_ref, v_ref, km_ref,
                   xe_ref, hc_ref):
    t = pl.program_id(1)
    tm = TM_MIX

    @pl.when(t == 0)
    def _():
        xe_ref[0:SUBLANES, :] = jnp.zeros((SUBLANES, LRU_WIDTH), F32)
        hc_ref[...] = jnp.zeros((SUBLANES, LRU_WIDTH), F32)

    h = _rms(x_ref[...], ng_ref[...]).astype(BF16)

    xr = _dot(h, win_ref[:, 0:LRU_WIDTH])
    xe_ref[SUBLANES:SUBLANES + tm, :] = xr
    xc = cb_ref[...]
    for j in range(LRU_CONV):
        off = SUBLANES - (LRU_CONV - 1) + j
        xc = xc + cw_ref[j:j + 1, :] * xe_ref[off:off + tm, :]
    xe_ref[0:SUBLANES, :] = xr[tm - SUBLANES:tm, :]

    xcb = xc.astype(BF16)
    half = LRU_WIDTH // 2
    r = jnp.concatenate([_dot(xcb[:, g * half:(g + 1) * half], wa_ref[g]) for g in range(2)], axis=-1)
    i = jnp.concatenate([_dot(xcb[:, g * half:(g + 1) * half], wi_ref[g]) for g in range(2)], axis=-1)
    r = jax.nn.sigmoid(r + ba_ref[...])
    i = jax.nn.sigmoid(i + bi_ref[...])
    nlam = -lam_ref[...]
    softplus = jnp.maximum(nlam, 0.0) + jnp.log1p(jnp.exp(-jnp.abs(nlam)))
    log_a = (-LRU_C * r) * softplus
    a = jnp.exp(log_a)
    u = jnp.sqrt(1.0 - jnp.exp(2.0 * log_a)) * (i * xc)

    row = lax.broadcasted_iota(jnp.int32, (SUBLANES, LRU_WIDTH), 0)
    carry = hc_ref[SUBLANES - 1:SUBLANES, :]
    slabs = []
    for g in range(tm // SUBLANES):
        ag = a[g * SUBLANES:(g + 1) * SUBLANES, :]
        hg = u[g * SUBLANES:(g + 1) * SUBLANES, :]
        for d in (1, 2, 4):
            keep = row >= d
            a_sh = jnp.where(keep, pltpu.roll(ag, d, 0), 1.0)
            h_sh = jnp.where(keep, pltpu.roll(hg, d, 0), 0.0)
            hg = hg + ag * h_sh
            ag = ag * a_sh
        hg = hg + ag * carry
        carry = hg[SUBLANES - 1:SUBLANES, :]
        slabs.append(hg)
    hc_ref[...] = slabs[-1]
    hs = jnp.concatenate(slabs, axis=0)

    yg = _dot(h, win_ref[:, LRU_WIDTH:2 * LRU_WIDTH])
    ylru_ref[...] = _rms(hs * jax.nn.gelu(yg), og_ref[...]).astype(BF16)

    ri = lax.broadcasted_iota(jnp.int32, (LANES, LANES), 0) >> 6
    ci = lax.broadcasted_iota(jnp.int32, (LANES, LANES), 1) >> 6
    head_ones = jnp.where(ri == ci, 1.0, 0.0).astype(BF16)

    def head_norm(z, g):
        sq = (z * z).astype(BF16)
        ssq = jnp.concatenate(
            [_dot(sq[:, c * LANES:(c + 1) * LANES], head_ones) for c in range(ATTN_WIDTH // LANES)], axis=-1)
        return z * lax.rsqrt(ssq * (1.0 / HEAD_DIM) + EPS) * g

    base = 2 * LRU_WIDTH
    q = _dot(h, win_ref[:, base:base + ATTN_WIDTH])
    q_ref[...] = (head_norm(q, qg_ref[...]) * (HEAD_DIM ** -0.5)).astype(BF16)
    k = _dot(h, win_ref[:, base + ATTN_WIDTH:base + 2 * ATTN_WIDTH])
    kn = head_norm(k, kg_ref[...])
    k_ref[...] = kn.astype(BF16)
    km_ref[...] = jnp.mean(kn, axis=0, keepdims=True)
    v = _dot(h, win_ref[:, base + 2 * ATTN_WIDTH:base + 3 * ATTN_WIDTH])
    v_ref[...] = v.astype(BF16)


def _mix_in(x, ng, win, cw, cb, wa, ba, wi, bi, lam, qg, kg, og):
    bsz, seq, _ = x.shape
    nt = seq // TM_MIX
    full = lambda shape: pl.BlockSpec(shape, lambda b, t: (0,) * len(shape))
    tile = lambda w: pl.BlockSpec((None, TM_MIX, w), lambda b, t: (b, t, 0))
    return pl.pallas_call(
        _mix_in_kernel,
        grid=(bsz, nt),
        in_specs=[
            tile(D_MODEL),
            full((1, D_MODEL)),
            full((D_MODEL, IN_WIDTH)),
            full((LRU_CONV, LRU_WIDTH)),
            full((1, LRU_WIDTH)),
            full((2, LRU_WIDTH // 2, LRU_WIDTH // 2)),
            full((1, LRU_WIDTH)),
            full((2, LRU_WIDTH // 2, LRU_WIDTH // 2)),
            full((1, LRU_WIDTH)),
            full((1, LRU_WIDTH)),
            full((1, ATTN_WIDTH)),
            full((1, ATTN_WIDTH)),
            full((1, LRU_WIDTH)),
        ],
        out_specs=[
            tile(LRU_WIDTH), tile(ATTN_WIDTH), tile(ATTN_WIDTH), tile(ATTN_WIDTH),
            pl.BlockSpec((None, None, 1, ATTN_WIDTH), lambda b, t: (b, t, 0, 0)),
        ],
        out_shape=[
            jax.ShapeDtypeStruct((bsz, seq, LRU_WIDTH), BF16),
            jax.ShapeDtypeStruct((bsz, seq, ATTN_WIDTH), BF16),
            jax.ShapeDtypeStruct((bsz, seq, ATTN_WIDTH), BF16),
            jax.ShapeDtypeStruct((bsz, seq, ATTN_WIDTH), BF16),
            jax.ShapeDtypeStruct((bsz, nt, 1, ATTN_WIDTH), F32),
        ],
        scratch_shapes=[
            pltpu.VMEM((SUBLANES + TM_MIX, LRU_WIDTH), F32),
            pltpu.VMEM((SUBLANES, LRU_WIDTH), F32),
        ],
        compiler_params=pltpu.CompilerParams(
            dimension_semantics=("arbitrary", "arbitrary"), vmem_limit_bytes=VMEM_LIMIT),
        name="mix_in",
    )(x, ng, win, cw, cb, wa, ba, wi, bi, lam, qg, kg, og)


def _moba_kernel(q_ref, k_ref, v_ref, km_ref, o_ref, vt_ref, ke_ref, s_ref):
    pair = pl.program_id(1)
    blk = MOBA_BLOCK
    nb = km_ref.shape[0]
    seq = k_ref.shape[0]
    blk_shift = blk.bit_length() - 1

    @pl.when((pl.program_id(0) == 0) & (pair == 0))
    def _():
        krow = lax.broadcasted_iota(jnp.int32, (seq, LANES), 0)
        klane = lax.broadcasted_iota(jnp.int32, (seq, LANES), 1)
        kblk = krow >> blk_shift
        extra = jnp.where(klane == kblk, 1.0, 0.0)
        extra = jnp.where(klane == nb, (kblk << blk_shift).astype(F32), extra)
        extra = jnp.where(klane == nb + 1, (krow & (blk - 1)).astype(F32), extra)
        ke_ref[...] = extra.astype(BF16)

    vt_ref[...] = v_ref[...].astype(F32).T.astype(BF16)

    kmb = km_ref[...].astype(BF16)
    lane_head = lax.broadcasted_iota(jnp.int32, (blk, PAIR_WIDTH), 1) >> 6
    causal = (lax.broadcasted_iota(jnp.int32, (blk, blk), 1)
              >= lax.broadcasted_iota(jnp.int32, (blk, blk), 0))
    blk_row = lax.broadcasted_iota(jnp.int32, (nb, blk), 0)
    feat_row = lax.broadcasted_iota(jnp.int32, (SUBLANES, blk), 0)
    zero_rows = jnp.zeros((LANES - nb - SUBLANES, blk), F32)
    dim_head = lax.broadcasted_iota(jnp.int32, (PAIR_WIDTH, blk), 0) >> 6

    units = [(qi, hh) for qi in range(seq // blk) for hh in range(2)]
    groups = blk // SUBLANES
    qa_of, mx_of, sum_of, acc_of, out_of = {}, {}, {}, {}, {}

    def prep(u):
        qi, hh = units[u]
        head = 2 * pair + hh
        slope = lax.bitcast_convert_type(
            jnp.broadcast_to((126 - head) << 23, (SUBLANES, blk)).astype(jnp.int32), F32)
        q = q_ref[qi * blk:(qi + 1) * blk, :]
        qh = jnp.where(lane_head == hh, q, jnp.zeros_like(q))
        gate = _dot_nt(kmb, qh)
        rank = jnp.zeros((nb, blk), F32)
        for m in range(qi):
            gm = gate[m:m + 1, :]
            rank = rank + jnp.where((gm > gate) | ((gm == gate) & (blk_row > m)), 1.0, 0.0)
        keep = (blk_row == qi) | ((blk_row < qi) & (rank < float(MOBA_TOPK)))
        feat_t = jnp.concatenate(
            [jnp.where(keep, 0.0, MASK_NEG), jnp.where(feat_row < 2, slope, 0.0), zero_rows], axis=0)
        qa_of[u] = jnp.concatenate([qh, feat_t.T.astype(BF16)], axis=1)

    def score_task(u, c):
        qi, _ = units[u]
        ka = jnp.concatenate([k_ref[c * blk:(c + 1) * blk, :], ke_ref[c * blk:(c + 1) * blk, :]], axis=1)
        s = _dot_nt(ka, qa_of[u])
        if c == qi:
            s = jnp.where(causal, s, MASK_NEG)
        s_ref[u % 2, c * blk:(c + 1) * blk, :] = s
        part = jnp.max(s.reshape(groups, SUBLANES, blk), axis=0)
        mx_of[u] = part if c == 0 else jnp.maximum(mx_of[u], part)

    def prob_task(u, c):
        if c == 0:
            mx_of[u] = jnp.max(mx_of[u], axis=0, keepdims=True)
        p = jnp.exp(s_ref[u % 2, c * blk:(c + 1) * blk, :] - mx_of[u])
        part = jnp.sum(p.reshape(groups, SUBLANES, blk), axis=0)
        pv = _dot(vt_ref[:, c * blk:(c + 1) * blk], p.astype(BF16))
        sum_of[u] = part if c == 0 else sum_of[u] + part
        acc_of[u] = pv if c == 0 else acc_of[u] + pv

    def finish(u):
        qi, hh = units[u]
        out_of[hh] = acc_of.pop(u) / jnp.sum(sum_of.pop(u), axis=0, keepdims=True)
        if hh == 1:
            out_t = jnp.where(dim_head == 0, out_of[0], out_of[1])
            o_ref[qi * blk:(qi + 1) * blk, :] = out_t.T

    n_units = len(units)
    prep(0)
    prep(1)
    for c in range(units[0][0] + 1):
        score_task(0, c)
    for u in range(n_units):
        if u + 2 < n_units:
            prep(u + 2)
        n_prob = units[u][0] + 1
        n_score = units[u + 1][0] + 1 if u + 1 < n_units else 0
        for c in range(max(n_prob, n_score)):
            if c < n_score:
                score_task(u + 1, c)
            if c < n_prob:
                prob_task(u, c)
        finish(u)


def _moba(q, k, v, kmean):
    bsz, seq, _ = q.shape
    nb = seq // MOBA_BLOCK
    col = lambda rows: pl.BlockSpec((None, rows, PAIR_WIDTH), lambda b, p: (b, 0, p))
    return pl.pallas_call(
        _moba_kernel,
        grid=(bsz, N_PAIRS),
        in_specs=[col(seq), col(seq), col(seq), col(nb)],
        out_specs=col(seq),
        out_shape=jax.ShapeDtypeStruct((bsz, seq, ATTN_WIDTH), F32),
        scratch_shapes=[
            pltpu.VMEM((PAIR_WIDTH, seq), BF16),
            pltpu.VMEM((seq, LANES), BF16),
            pltpu.VMEM((2, seq, MOBA_BLOCK), F32),
        ],
        compiler_params=pltpu.CompilerParams(
            dimension_semantics=("arbitrary", "arbitrary"), vmem_limit_bytes=VMEM_LIMIT),
        name="moba",
    )(q, k, v, kmean)


def _ffn_kernel(x_ref, yl_ref, ya_ref, ag_ref, wout_ref, fg_ref, wup_ref, cw_ref, cb_ref, wdn_ref,
                o_ref, hf_ref, up_ref):
    t = pl.program_id(1)
    tm = TM_FFN
    halo = BF16_ROWS

    @pl.when(t == 0)
    def _():
        hf_ref[0:halo, :] = jnp.zeros((halo, D_MODEL), BF16)

    an = _rms(ya_ref[...], ag_ref[...]).astype(BF16)
    x1 = (x_ref[...] + _dot(yl_ref[...], wout_ref[0:LRU_WIDTH, :])
          + _dot(an, wout_ref[LRU_WIDTH:LRU_WIDTH + ATTN_WIDTH, :]))
    hf = _rms(x1, fg_ref[...]).astype(BF16)
    hf_ref[halo:halo + tm, :] = hf

    def chunk(c, acc):
        off_g = pl.multiple_of(c * TF, TF)
        off_v = pl.multiple_of(D_FF + c * TF, LANES)
        hfe = hf_ref[...]
        up_ref[:, 0:TF] = _dot(hfe, wup_ref[:, pl.ds(off_g, TF)])
        up_ref[:, TF:2 * TF] = _dot(hfe, wup_ref[:, pl.ds(off_v, TF)])

        def conv(col0, off):
            w = cw_ref[:, pl.ds(off, TF)]
            y = cb_ref[:, pl.ds(off, TF)]
            for j in range(FFN_CONV):
                r0 = halo - (FFN_CONV - 1) + j
                y = y + w[j:j + 1, :] * up_ref[r0:r0 + tm, col0:col0 + TF]
            return y

        act = (jax.nn.gelu(conv(0, off_g)) * conv(TF, off_v)).astype(BF16)
        return acc + _dot(act, wdn_ref[pl.ds(off_g, TF), :])

    o_ref[...] = lax.fori_loop(0, D_FF // TF, chunk, x1)
    hf_ref[0:halo, :] = hf[tm - halo:tm, :]


def _ffn(x, ylru, yattn, ag, wout, fg, wup, cw, cb, wdn):
    bsz, seq, _ = x.shape
    full = lambda shape: pl.BlockSpec(shape, lambda b, t: (0,) * len(shape))
    tile = lambda w: pl.BlockSpec((None, TM_FFN, w), lambda b, t: (b, t, 0))
    return pl.pallas_call(
        _ffn_kernel,
        grid=(bsz, seq // TM_FFN),
        in_specs=[
            tile(D_MODEL), tile(LRU_WIDTH), tile(ATTN_WIDTH),
            full((1, ATTN_WIDTH)),
            full((D_MODEL, D_MODEL)),
            full((1, D_MODEL)),
            full((D_MODEL, 2 * D_FF)),
            full((FFN_CONV, 2 * D_FF)),
            full((1, 2 * D_FF)),
            full((D_FF, D_MODEL)),
        ],
        out_specs=tile(D_MODEL),
        out_shape=jax.ShapeDtypeStruct((bsz, seq, D_MODEL), F32),
        scratch_shapes=[
            pltpu.VMEM((BF16_ROWS + TM_FFN, D_MODEL), BF16),
            pltpu.VMEM((BF16_ROWS + TM_FFN, 2 * TF), F32),
        ],
        compiler_params=pltpu.CompilerParams(
            dimension_semantics=("arbitrary", "arbitrary"), vmem_limit_bytes=VMEM_LIMIT),
        name="ffn",
    )(x, ylru, yattn, ag, wout, fg, wup, cw, cb, wdn)


def _block_diag_halves(w):
    hd = w.shape[-1]
    w4 = w.reshape(2, LRU_HEADS // 2, hd, hd)
    eye = jnp.eye(LRU_HEADS // 2, dtype=w.dtype)
    return jnp.einsum('ghij,hk->ghikj', w4, eye).reshape(2, LRU_WIDTH // 2, LRU_WIDTH // 2)


def kernel(x, mix_norm_g, w_in, lru_conv_w, lru_conv_b, lru_w_a, lru_b_a, lru_w_i, lru_b_i, lru_lambda,
           q_norm_g, k_norm_g, lru_out_g, attn_out_g, w_out, ffn_norm_g, w_up, ffn_conv_w, ffn_conv_b, w_down):
    depth = w_in.shape[0]
    bsz, seq, _ = x.shape
    assert seq % MOBA_BLOCK == 0 and seq // MOBA_BLOCK > MOBA_TOPK
    row = lambda p: p.reshape(1, -1)
    for layer in range(depth):
        ylru, q, k, v, kmean = _mix_in(
            x, row(mix_norm_g[layer]), w_in[layer].astype(BF16),
            lru_conv_w[layer], row(lru_conv_b[layer]),
            _block_diag_halves(lru_w_a[layer]).astype(BF16), row(lru_b_a[layer]),
            _block_diag_halves(lru_w_i[layer]).astype(BF16), row(lru_b_i[layer]),
            row(lru_lambda[layer]),
            row(jnp.tile(q_norm_g[layer], ATTN_HEADS)), row(jnp.tile(k_norm_g[layer], ATTN_HEADS)),
            row(lru_out_g[layer]))
        yattn = _moba(q, k, v, kmean.reshape(bsz, seq // MOBA_BLOCK, ATTN_WIDTH))
        x = _ffn(x, ylru, yattn, row(attn_out_g[layer]), w_out[layer].astype(BF16),
                 row(ffn_norm_g[layer]), w_up[layer].astype(BF16),
                 ffn_conv_w[layer], row(ffn_conv_b[layer]), w_down[layer].astype(BF16))
    return x
```

```python
import functools

import jax
import jax.numpy as jnp
from jax import lax
from jax.experimental import pallas as pl
from jax.experimental.pallas import tpu as pltpu

F32 = jnp.float32
BF16 = jnp.bfloat16

D_MODEL = 1024
LRU_WIDTH = 512
LRU_HEADS = 8
LRU_CONV = 4
LRU_C = 8.0
ATTN_WIDTH = 512
ATTN_HEADS = 8
HEAD_DIM = 64
IN_WIDTH = 2 * LRU_WIDTH + 3 * ATTN_WIDTH
MOBA_BLOCK = 256
MOBA_TOPK = 3
D_FF = 2816
FFN_CONV = 3
EPS = 1e-6

LANES = 128
SUBLANES = 8
BF16_ROWS = 16
PAIR_WIDTH = 2 * HEAD_DIM
N_PAIRS = ATTN_HEADS // 2
MASK_NEG = -1e30

TM_MIX = MOBA_BLOCK
TM_FFN = 512
TF = 256
VMEM_LIMIT = 56 * 1024 * 1024


def _rms(x, g):
    return x * lax.rsqrt(jnp.mean(x * x, axis=-1, keepdims=True) + EPS) * g


def _dot(a, b):
    return jnp.dot(a, b, preferred_element_type=F32)


def _dot_nt(a, b):
    return lax.dot_general(a, b, (((1,), (1,)), ((), ())), preferred_element_type=F32)


def _mix_in_kernel(x_ref, ng_ref, win_ref, cw_ref, cb_ref, wa_ref, ba_ref, wi_ref, bi_ref,
                   lam_ref, qg_ref, kg_ref, og_ref,
                   ylru_ref, q_ref, k_ref, v_ref, km_ref,
                   xe_ref, hc_ref):
    t = pl.program_id(1)
    tm = TM_MIX

    @pl.when(t == 0)
    def _():
        xe_ref[0:SUBLANES, :] = jnp.zeros((SUBLANES, LRU_WIDTH), F32)
        hc_ref[...] = jnp.zeros((SUBLANES, LRU_WIDTH), F32)

    h = _rms(x_ref[...], ng_ref[...]).astype(BF16)

    xr = _dot(h, win_ref[:, 0:LRU_WIDTH])
    xe_ref[SUBLANES:SUBLANES + tm, :] = xr
    xc = cb_ref[...]
    for j in range(LRU_CONV):
        off = SUBLANES - (LRU_CONV - 1) + j
        xc = xc + cw_ref[j:j + 1, :] * xe_ref[off:off + tm, :]
    xe_ref[0:SUBLANES, :] = xr[tm - SUBLANES:tm, :]

    xcb = xc.astype(BF16)
    half = LRU_WIDTH // 2
    r = jnp.concatenate([_dot(xcb[:, g * half:(g + 1) * half], wa_ref[g]) for g in range(2)], axis=-1)
    i = jnp.concatenate([_dot(xcb[:, g * half:(g + 1) * half], wi_ref[g]) for g in range(2)], axis=-1)
    r = jax.nn.sigmoid(r + ba_ref[...])
    i = jax.nn.sigmoid(i + bi_ref[...])
    nlam = -lam_ref[...]
    softplus = jnp.maximum(nlam, 0.0) + jnp.log1p(jnp.exp(-jnp.abs(nlam)))
    log_a = (-LRU_C * r) * softplus
    a = jnp.exp(log_a)
    one_m_a2 = 1.0 - a * a
    root = jnp.where(one_m_a2 > 0.0, one_m_a2 * lax.rsqrt(one_m_a2), 0.0)
    u = root * (i * xc)

    row = lax.broadcasted_iota(jnp.int32, (SUBLANES, LRU_WIDTH), 0)
    carry = hc_ref[SUBLANES - 1:SUBLANES, :]
    slabs = []
    for g in range(tm // SUBLANES):
        ag = a[g * SUBLANES:(g + 1) * SUBLANES, :]
        hg = u[g * SUBLANES:(g + 1) * SUBLANES, :]
        for d in (1, 2, 4):
            keep = row >= d
            a_sh = jnp.where(keep, pltpu.roll(ag, d, 0), 1.0)
            h_sh = jnp.where(keep, pltpu.roll(hg, d, 0), 0.0)
            hg = hg + ag * h_sh
            ag = ag * a_sh
        hg = hg + ag * carry
        carry = hg[SUBLANES - 1:SUBLANES, :]
        slabs.append(hg)
    hc_ref[...] = slabs[-1]
    hs = jnp.concatenate(slabs, axis=0)

    yg = _dot(h, win_ref[:, LRU_WIDTH:2 * LRU_WIDTH])
    ylru_ref[...] = _rms(hs * jax.nn.gelu(yg), og_ref[...]).astype(BF16)

    ri = lax.broadcasted_iota(jnp.int32, (LANES, LANES), 0) >> 6
    ci = lax.broadcasted_iota(jnp.int32, (LANES, LANES), 1) >> 6
    head_ones = jnp.where(ri == ci, 1.0, 0.0).astype(BF16)

    def head_norm(z, g):
        sq = (z * z).astype(BF16)
        ssq = jnp.concatenate(
            [_dot(sq[:, c * LANES:(c + 1) * LANES], head_ones) for c in range(ATTN_WIDTH // LANES)], axis=-1)
        return z * lax.rsqrt(ssq * (1.0 / HEAD_DIM) + EPS) * g

    base = 2 * LRU_WIDTH
    q = _dot(h, win_ref[:, base:base + ATTN_WIDTH])
    q_ref[...] = (head_norm(q, qg_ref[...]) * (HEAD_DIM ** -0.5)).astype(BF16)
    k = _dot(h, win_ref[:, base + ATTN_WIDTH:base + 2 * ATTN_WIDTH])
    kn = head_norm(k, kg_ref[...])
    k_ref[...] = kn.astype(BF16)
    km_ref[...] = jnp.mean(kn, axis=0, keepdims=True)
    v = _dot(h, win_ref[:, base + 2 * ATTN_WIDTH:base + 3 * ATTN_WIDTH])
    v_ref[...] = v.astype(BF16)


def _mix_in(x, ng, win, cw, cb, wa, ba, wi, bi, lam, qg, kg, og):
    bsz, seq, _ = x.shape
    nt = seq // TM_MIX
    full = lambda shape: pl.BlockSpec(shape, lambda b, t: (0,) * len(shape))
    tile = lambda w: pl.BlockSpec((None, TM_MIX, w), lambda b, t: (b, t, 0))
    return pl.pallas_call(
        _mix_in_kernel,
        grid=(bsz, nt),
        in_specs=[
            tile(D_MODEL),
            full((1, D_MODEL)),
            full((D_MODEL, IN_WIDTH)),
            full((LRU_CONV, LRU_WIDTH)),
            full((1, LRU_WIDTH)),
            full((2, LRU_WIDTH // 2, LRU_WIDTH // 2)),
            full((1, LRU_WIDTH)),
            full((2, LRU_WIDTH // 2, LRU_WIDTH // 2)),
            full((1, LRU_WIDTH)),
            full((1, LRU_WIDTH)),
            full((1, ATTN_WIDTH)),
            full((1, ATTN_WIDTH)),
            full((1, LRU_WIDTH)),
        ],
        out_specs=[
            tile(LRU_WIDTH), tile(ATTN_WIDTH), tile(ATTN_WIDTH), tile(ATTN_WIDTH),
            pl.BlockSpec((None, None, 1, ATTN_WIDTH), lambda b, t: (b, t, 0, 0)),
        ],
        out_shape=[
            jax.ShapeDtypeStruct((bsz, seq, LRU_WIDTH), BF16),
            jax.ShapeDtypeStruct((bsz, seq, ATTN_WIDTH), BF16),
            jax.ShapeDtypeStruct((bsz, seq, ATTN_WIDTH), BF16),
            jax.ShapeDtypeStruct((bsz, seq, ATTN_WIDTH), BF16),
            jax.ShapeDtypeStruct((bsz, nt, 1, ATTN_WIDTH), F32),
        ],
        scratch_shapes=[
            pltpu.VMEM((SUBLANES + TM_MIX, LRU_WIDTH), F32),
            pltpu.VMEM((SUBLANES, LRU_WIDTH), F32),
        ],
        compiler_params=pltpu.CompilerParams(
            dimension_semantics=("arbitrary", "arbitrary"), vmem_limit_bytes=VMEM_LIMIT),
        name="mix_in",
    )(x, ng, win, cw, cb, wa, ba, wi, bi, lam, qg, kg, og)


def _moba_kernel(q_ref, k_ref, v_ref, km_ref, o_ref, vt_ref, ke_ref, s_ref):
    pair = pl.program_id(1)
    blk = MOBA_BLOCK
    nb = km_ref.shape[0]
    seq = k_ref.shape[0]
    blk_shift = blk.bit_length() - 1

    @pl.when((pl.program_id(0) == 0) & (pair == 0))
    def _():
        krow = lax.broadcasted_iota(jnp.int32, (seq, LANES), 0)
        klane = lax.broadcasted_iota(jnp.int32, (seq, LANES), 1)
        kblk = krow >> blk_shift
        extra = jnp.where(klane == kblk, 1.0, 0.0)
        extra = jnp.where(klane == nb, (kblk << blk_shift).astype(F32), extra)
        extra = jnp.where(klane == nb + 1, (krow & (blk - 1)).astype(F32), extra)
        ke_ref[...] = extra.astype(BF16)

    vt_ref[...] = v_ref[...].astype(F32).T.astype(BF16)

    kmb = km_ref[...].astype(BF16)
    lane_head = lax.broadcasted_iota(jnp.int32, (blk, PAIR_WIDTH), 1) >> 6
    causal = (lax.broadcasted_iota(jnp.int32, (blk, blk), 1)
              >= lax.broadcasted_iota(jnp.int32, (blk, blk), 0))
    blk_row = lax.broadcasted_iota(jnp.int32, (nb, blk), 0)
    feat_row = lax.broadcasted_iota(jnp.int32, (SUBLANES, blk), 0)
    zero_rows = jnp.zeros((LANES - nb - SUBLANES, blk), F32)
    dim_head = lax.broadcasted_iota(jnp.int32, (PAIR_WIDTH, blk), 0) >> 6

    units = [(qi, hh) for qi in range(seq // blk) for hh in range(2)]
    groups = blk // SUBLANES
    qa_of, mx_of, sum_of, acc_of, out_of = {}, {}, {}, {}, {}

    def prep(u):
        qi, hh = units[u]
        head = 2 * pair + hh
        slope = lax.bitcast_convert_type(
            jnp.broadcast_to((126 - head) << 23, (SUBLANES, blk)).astype(jnp.int32), F32)
        q = q_ref[qi * blk:(qi + 1) * blk, :]
        qh = jnp.where(lane_head == hh, q, jnp.zeros_like(q))
        gate = _dot_nt(kmb, qh)
        rank = jnp.zeros((nb, blk), F32)
        for m in range(qi):
            gm = gate[m:m + 1, :]
            rank = rank + jnp.where((gm > gate) | ((gm == gate) & (blk_row > m)), 1.0, 0.0)
        keep = (blk_row == qi) | ((blk_row < qi) & (rank < float(MOBA_TOPK)))
        feat_t = jnp.concatenate(
            [jnp.where(keep, 0.0, MASK_NEG), jnp.where(feat_row < 2, slope, 0.0), zero_rows], axis=0)
        qa_of[u] = jnp.concatenate([qh, feat_t.T.astype(BF16)], axis=1)

    def score_task(u, c):
        qi, _ = units[u]
        ka = jnp.concatenate([k_ref[c * blk:(c + 1) * blk, :], ke_ref[c * blk:(c + 1) * blk, :]], axis=1)
        s = _dot_nt(ka, qa_of[u])
        if c == qi:
            s = jnp.where(causal, s, MASK_NEG)
        s_ref[u % 2, c * blk:(c + 1) * blk, :] = s
        part = jnp.max(s.reshape(groups, SUBLANES, blk), axis=0)
        mx_of[u] = part if c == 0 else jnp.maximum(mx_of[u], part)

    def prob_task(u, c):
        if c == 0:
            mx_of[u] = jnp.max(mx_of[u], axis=0, keepdims=True)
        p = jnp.exp(s_ref[u % 2, c * blk:(c + 1) * blk, :] - mx_of[u])
        part = jnp.sum(p.reshape(groups, SUBLANES, blk), axis=0)
        pv = _dot(vt_ref[:, c * blk:(c + 1) * blk], p.astype(BF16))
        sum_of[u] = part if c == 0 else sum_of[u] + part
        acc_of[u] = pv if c == 0 else acc_of[u] + pv

    def finish(u):
        qi, hh = units[u]
        out_of[hh] = acc_of.pop(u) / jnp.sum(sum_of.pop(u), axis=0, keepdims=True)
        if hh == 1:
            out_t = jnp.where(dim_head == 0, out_of[0], out_of[1])
            o_ref[qi * blk:(qi + 1) * blk, :] = out_t.T

    n_units = len(units)
    prep(0)
    prep(1)
    for c in range(units[0][0] + 1):
        score_task(0, c)
    for u in range(n_units):
        if u + 2 < n_units:
            prep(u + 2)
        n_prob = units[u][0] + 1
        n_score = units[u + 1][0] + 1 if u + 1 < n_units else 0
        for c in range(max(n_prob, n_score)):
            if c < n_score:
                score_task(u + 1, c)
            if c < n_prob:
                prob_task(u, c)
        finish(u)


def _moba(q, k, v, kmean):
    bsz, seq, _ = q.shape
    nb = seq // MOBA_BLOCK
    col = lambda rows: pl.BlockSpec((None, rows, PAIR_WIDTH), lambda b, p: (b, 0, p))
    return pl.pallas_call(
        _moba_kernel,
        grid=(bsz, N_PAIRS),
        in_specs=[col(seq), col(seq), col(seq), col(nb)],
        out_specs=col(seq),
        out_shape=jax.ShapeDtypeStruct((bsz, seq, ATTN_WIDTH), F32),
        scratch_shapes=[
            pltpu.VMEM((PAIR_WIDTH, seq), BF16),
            pltpu.VMEM((seq, LANES), BF16),
            pltpu.VMEM((2, seq, MOBA_BLOCK), F32),
        ],
        compiler_params=pltpu.CompilerParams(
            dimension_semantics=("arbitrary", "arbitrary"), vmem_limit_bytes=VMEM_LIMIT),
        name="moba",
    )(q, k, v, kmean)


def _ffn_kernel(x_ref, yl_ref, ya_ref, ag_ref, wout_ref, fg_ref, wup_ref, cw_ref, cb_ref, wdn_ref,
                o_ref, hf_ref, up_ref, act_ref):
    t = pl.program_id(1)
    tm = TM_FFN
    halo = BF16_ROWS
    n_chunks = D_FF // TF

    @pl.when(t == 0)
    def _():
        hf_ref[0:halo, :] = jnp.zeros((halo, D_MODEL), BF16)

    an = _rms(ya_ref[...], ag_ref[...]).astype(BF16)
    x1 = (x_ref[...] + _dot(yl_ref[...], wout_ref[0:LRU_WIDTH, :])
          + _dot(an, wout_ref[LRU_WIDTH:LRU_WIDTH + ATTN_WIDTH, :]))
    hf = _rms(x1, fg_ref[...]).astype(BF16)
    hf_ref[halo:halo + tm, :] = hf

    def up_task(c):
        hfe = hf_ref[...]
        up_ref[c % 2, :, 0:TF] = _dot(hfe, wup_ref[:, c * TF:(c + 1) * TF])
        up_ref[c % 2, :, TF:2 * TF] = _dot(hfe, wup_ref[:, D_FF + c * TF:D_FF + (c + 1) * TF])

    def act_task(c):
        def conv(col0, off):
            y = cb_ref[:, off:off + TF]
            for j in range(FFN_CONV):
                r0 = halo - (FFN_CONV - 1) + j
                y = y + cw_ref[j:j + 1, off:off + TF] * up_ref[c % 2, r0:r0 + tm, col0:col0 + TF]
            return y

        act = jax.nn.gelu(conv(0, c * TF)) * conv(TF, D_FF + c * TF)
        act_ref[:, c * TF:(c + 1) * TF] = act.astype(BF16)

    up_task(0)
    for c in range(n_chunks):
        if c + 1 < n_chunks:
            up_task(c + 1)
        act_task(c)

    o_ref[...] = x1 + _dot(act_ref[...], wdn_ref[...])
    hf_ref[0:halo, :] = hf[tm - halo:tm, :]


def _ffn(x, ylru, yattn, ag, wout, fg, wup, cw, cb, wdn):
    bsz, seq, _ = x.shape
    full = lambda shape: pl.BlockSpec(shape, lambda b, t: (0,) * len(shape), pipeline_mode=pl.Buffered(1))
    tile = lambda w: pl.BlockSpec((None, TM_FFN, w), lambda b, t: (b, t, 0))
    return pl.pallas_call(
        _ffn_kernel,
        grid=(bsz, seq // TM_FFN),
        in_specs=[
            tile(D_MODEL), tile(LRU_WIDTH), tile(ATTN_WIDTH),
            full((1, ATTN_WIDTH)),
            full((D_MODEL, D_MODEL)),
            full((1, D_MODEL)),
            full((D_MODEL, 2 * D_FF)),
            full((FFN_CONV, 2 * D_FF)),
            full((1, 2 * D_FF)),
            full((D_FF, D_MODEL)),
        ],
        out_specs=tile(D_MODEL),
        out_shape=jax.ShapeDtypeStruct((bsz, seq, D_MODEL), F32),
        scratch_shapes=[
            pltpu.VMEM((BF16_ROWS + TM_FFN, D_MODEL), BF16),
            pltpu.VMEM((2, BF16_ROWS + TM_FFN, 2 * TF), F32),
            pltpu.VMEM((TM_FFN, D_FF), BF16),
        ],
        compiler_params=pltpu.CompilerParams(
            dimension_semantics=("arbitrary", "arbitrary"), vmem_limit_bytes=VMEM_LIMIT),
        name="ffn",
    )(x, ylru, yattn, ag, wout, fg, wup, cw, cb, wdn)


def _block_diag_halves(w):
    hd = w.shape[-1]
    w4 = w.reshape(2, LRU_HEADS // 2, hd, hd)
    eye = jnp.eye(LRU_HEADS // 2, dtype=w.dtype)
    return jnp.einsum('ghij,hk->ghikj', w4, eye).reshape(2, LRU_WIDTH // 2, LRU_WIDTH // 2)


def kernel(x, mix_norm_g, w_in, lru_conv_w, lru_conv_b, lru_w_a, lru_b_a, lru_w_i, lru_b_i, lru_lambda,
           q_norm_g, k_norm_g, lru_out_g, attn_out_g, w_out, ffn_norm_g, w_up, ffn_conv_w, ffn_conv_b, w_down):
    depth = w_in.shape[0]
    bsz, seq, _ = x.shape
    assert seq % MOBA_BLOCK == 0 and seq // MOBA_BLOCK > MOBA_TOPK
    row = lambda p: p.reshape(1, -1)
    for layer in range(depth):
        ylru, q, k, v, kmean = _mix_in(
            x, row(mix_norm_g[layer]), w_in[layer].astype(BF16),
            lru_conv_w[layer], row(lru_conv_b[layer]),
            _block_diag_halves(lru_w_a[layer]).astype(BF16), row(lru_b_a[layer]),
            _block_diag_halves(lru_w_i[layer]).astype(BF16), row(lru_b_i[layer]),
            row(lru_lambda[layer]),
            row(jnp.tile(q_norm_g[layer], ATTN_HEADS)), row(jnp.tile(k_norm_g[layer], ATTN_HEADS)),
            row(lru_out_g[layer]))
        yattn = _moba(q, k, v, kmean.reshape(bsz, seq // MOBA_BLOCK, ATTN_WIDTH))
        x = _ffn(x, ylru, yattn, row(attn_out_g[layer]), w_out[layer].astype(BF16),
                 row(ffn_norm_g[layer]), w_up[layer].astype(BF16),
                 ffn_conv_w[layer], row(ffn_conv_b[layer]), w_down[layer].astype(BF16))
    return x
```

```python
import functools

import jax
import jax.numpy as jnp
from jax import lax
from jax.experimental import pallas as pl
from jax.experimental.pallas import tpu as pltpu

F32 = jnp.float32
BF16 = jnp.bfloat16

D_MODEL = 1024
LRU_WIDTH = 512
LRU_HEADS = 8
LRU_CONV = 4
LRU_C = 8.0
ATTN_WIDTH = 512
ATTN_HEADS = 8
HEAD_DIM = 64
IN_WIDTH = 2 * LRU_WIDTH + 3 * ATTN_WIDTH
MOBA_BLOCK = 256
MOBA_TOPK = 3
D_FF = 2816
FFN_CONV = 3
EPS = 1e-6

LANES = 128
SUBLANES = 8
BF16_ROWS = 16
PAIR_WIDTH = 2 * HEAD_DIM
N_PAIRS = ATTN_HEADS // 2
MASK_NEG = -1e30

TM_MIX = MOBA_BLOCK
TM_FFN = 512
TF = 256
VMEM_LIMIT = 56 * 1024 * 1024


def _rms(x, g):
    return x * lax.rsqrt(jnp.mean(x * x, axis=-1, keepdims=True) + EPS) * g


def _dot(a, b):
    return jnp.dot(a, b, preferred_element_type=F32)


PERM_GROUPS = MOBA_BLOCK // SUBLANES


STAGE_STRIDE = PERM_GROUPS + SUBLANES
STAGE_BLOCK_ROWS = SUBLANES * STAGE_STRIDE


def _load_time_permuted(ref, stage_ref, blocks):
    chunks = ref.shape[-1] // LANES
    for c in range(chunks):
        for b in range(blocks):
            for k in range(SUBLANES):
                src = b * MOBA_BLOCK + k * PERM_GROUPS
                dst = b * STAGE_BLOCK_ROWS + k * STAGE_STRIDE
                stage_ref[c, dst:dst + PERM_GROUPS, :] = ref[src:src + PERM_GROUPS, c * LANES:(c + 1) * LANES]
    return jnp.concatenate(
        [jnp.concatenate([stage_ref[c, pl.ds(b * STAGE_BLOCK_ROWS + g, SUBLANES, stride=STAGE_STRIDE), :]
                          for b in range(blocks) for g in range(PERM_GROUPS)], axis=0)
         for c in range(chunks)], axis=1)


def _store_time_unpermuted(ref, stage_ref, val, blocks):
    chunks = ref.shape[-1] // LANES
    for c in range(chunks):
        for b in range(blocks):
            for g in range(PERM_GROUPS):
                r0 = b * MOBA_BLOCK + g * SUBLANES
                stage_ref[c, pl.ds(b * STAGE_BLOCK_ROWS + g, SUBLANES, stride=STAGE_STRIDE), :] = (
                    val[r0:r0 + SUBLANES, c * LANES:(c + 1) * LANES])
    for c in range(chunks):
        for b in range(blocks):
            for k in range(SUBLANES):
                dst = b * MOBA_BLOCK + k * PERM_GROUPS
                src = b * STAGE_BLOCK_ROWS + k * STAGE_STRIDE
                ref[dst:dst + PERM_GROUPS, c * LANES:(c + 1) * LANES] = stage_ref[c, src:src + PERM_GROUPS, :]


def _time_offset(row):
    group = (row & (MOBA_BLOCK - 1)) >> (SUBLANES.bit_length() - 1)
    return group + (row & (SUBLANES - 1)) * PERM_GROUPS


def _prev_step_group(before, group, row):
    return jnp.where(row == 0, pltpu.roll(before, 1, 0), pltpu.roll(group, 1, 0))


def _dot_nt(a, b):
    return lax.dot_general(a, b, (((1,), (1,)), ((), ())), preferred_element_type=F32)


def _mix_in_kernel(x_ref, ng_ref, win_ref, cw_ref, cb_ref, wa_ref, ba_ref, wi_ref, bi_ref,
                   lam_ref, qg_ref, kg_ref, og_ref,
                   ylru_ref, q_ref, k_ref, v_ref, km_ref,
                   xe_ref, hc_ref, stage_ref):
    t = pl.program_id(1)
    tm = TM_MIX
    tail = (LRU_CONV - 1) * SUBLANES

    @pl.when(t == 0)
    def _():
        xe_ref[...] = jnp.zeros((tail, LRU_WIDTH), F32)
        hc_ref[...] = jnp.zeros((SUBLANES, LRU_WIDTH), F32)

    h = _rms(_load_time_permuted(x_ref, stage_ref, 1), ng_ref[...]).astype(BF16)
    row = lax.broadcasted_iota(jnp.int32, (SUBLANES, LRU_WIDTH), 0)

    xr = _dot(h, win_ref[:, 0:LRU_WIDTH])
    base = 2 * LRU_WIDTH
    v_ref[...] = _dot(h, win_ref[:, base + 2 * ATTN_WIDTH:base + 3 * ATTN_WIDTH]).astype(BF16)
    prefix = [_prev_step_group(xe_ref[g * SUBLANES:(g + 1) * SUBLANES, :],
                               xr[tm - tail + g * SUBLANES:tm - tail + (g + 1) * SUBLANES, :], row)
              for g in range(LRU_CONV - 1)]
    ext = jnp.concatenate(prefix + [xr], axis=0)
    xc = cb_ref[...]
    for j in range(LRU_CONV):
        xc = xc + cw_ref[j:j + 1, :] * ext[j * SUBLANES:j * SUBLANES + tm, :]
    xe_ref[...] = xr[tm - tail:tm, :]

    xcb = xc.astype(BF16)
    half = LRU_WIDTH // 2
    r = jnp.concatenate([_dot(xcb[:, g * half:(g + 1) * half], wa_ref[g]) for g in range(2)], axis=-1)
    i = jnp.concatenate([_dot(xcb[:, g * half:(g + 1) * half], wi_ref[g]) for g in range(2)], axis=-1)
    q = _dot(h, win_ref[:, base:base + ATTN_WIDTH])
    k = _dot(h, win_ref[:, base + ATTN_WIDTH:base + 2 * ATTN_WIDTH])
    yg = _dot(h, win_ref[:, LRU_WIDTH:2 * LRU_WIDTH])
    r = jax.nn.sigmoid(r + ba_ref[...])
    i = jax.nn.sigmoid(i + bi_ref[...])
    nlam = -lam_ref[...]
    softplus = jnp.maximum(nlam, 0.0) + jnp.log1p(jnp.exp(-jnp.abs(nlam)))
    log_a = (-LRU_C * r) * softplus
    a = jnp.exp(log_a)
    one_m_a2 = 1.0 - a * a
    root = jnp.where(one_m_a2 > 0.0, one_m_a2 * lax.rsqrt(one_m_a2), 0.0)
    u = root * (i * xc)

    groups = tm // SUBLANES
    h_loc = [u[0:SUBLANES, :]]
    a_cum = [a[0:SUBLANES, :]]
    for g in range(1, groups):
        ag = a[g * SUBLANES:(g + 1) * SUBLANES, :]
        h_loc.append(ag * h_loc[-1] + u[g * SUBLANES:(g + 1) * SUBLANES, :])
        a_cum.append(ag * a_cum[-1])
    end_h, end_a = h_loc[-1], a_cum[-1]
    for d in (1, 2, 4):
        keep = row >= d
        a_sh = jnp.where(keep, pltpu.roll(end_a, d, 0), 1.0)
        h_sh = jnp.where(keep, pltpu.roll(end_h, d, 0), 0.0)
        end_h = end_h + end_a * h_sh
        end_a = end_a * a_sh
    carry = hc_ref[SUBLANES - 1:SUBLANES, :]
    chain_end = end_h + end_a * carry
    chain_start = jnp.where(row == 0, carry, pltpu.roll(chain_end, 1, 0))
    hc_ref[...] = chain_end
    hs = jnp.concatenate([h_loc[g] + a_cum[g] * chain_start for g in range(groups)], axis=0)

    ylru_ref[...] = _rms(hs * jax.nn.gelu(yg), og_ref[...]).astype(BF16)

    ri = lax.broadcasted_iota(jnp.int32, (LANES, LANES), 0) >> 6
    ci = lax.broadcasted_iota(jnp.int32, (LANES, LANES), 1) >> 6
    head_ones = jnp.where(ri == ci, 1.0, 0.0).astype(BF16)

    def head_norm(z, g):
        sq = (z * z).astype(BF16)
        ssq = jnp.concatenate(
            [_dot(sq[:, c * LANES:(c + 1) * LANES], head_ones) for c in range(ATTN_WIDTH // LANES)], axis=-1)
        return z * lax.rsqrt(ssq * (1.0 / HEAD_DIM) + EPS) * g

    q_ref[...] = (head_norm(q, qg_ref[...]) * (HEAD_DIM ** -0.5)).astype(BF16)
    kn = head_norm(k, kg_ref[...])
    k_ref[...] = kn.astype(BF16)
    km_ref[...] = jnp.mean(kn, axis=0, keepdims=True)


def _mix_in(x, ng, win, cw, cb, wa, ba, wi, bi, lam, qg, kg, og):
    bsz, seq, _ = x.shape
    nt = seq // TM_MIX
    full = lambda shape: pl.BlockSpec(shape, lambda b, t: (0,) * len(shape))
    tile = lambda w: pl.BlockSpec((None, TM_MIX, w), lambda b, t: (b, t, 0))
    return pl.pallas_call(
        _mix_in_kernel,
        grid=(bsz, nt),
        in_specs=[
            tile(D_MODEL),
            full((1, D_MODEL)),
            full((D_MODEL, IN_WIDTH)),
            full((LRU_CONV, LRU_WIDTH)),
            full((1, LRU_WIDTH)),
            full((2, LRU_WIDTH // 2, LRU_WIDTH // 2)),
            full((1, LRU_WIDTH)),
            full((2, LRU_WIDTH // 2, LRU_WIDTH // 2)),
            full((1, LRU_WIDTH)),
            full((1, LRU_WIDTH)),
            full((1, ATTN_WIDTH)),
            full((1, ATTN_WIDTH)),
            full((1, LRU_WIDTH)),
        ],
        out_specs=[
            tile(LRU_WIDTH), tile(ATTN_WIDTH), tile(ATTN_WIDTH), tile(ATTN_WIDTH),
            pl.BlockSpec((None, None, 1, ATTN_WIDTH), lambda b, t: (b, t, 0, 0)),
        ],
        out_shape=[
            jax.ShapeDtypeStruct((bsz, seq, LRU_WIDTH), BF16),
            jax.ShapeDtypeStruct((bsz, seq, ATTN_WIDTH), BF16),
            jax.ShapeDtypeStruct((bsz, seq, ATTN_WIDTH), BF16),
            jax.ShapeDtypeStruct((bsz, seq, ATTN_WIDTH), BF16),
            jax.ShapeDtypeStruct((bsz, nt, 1, ATTN_WIDTH), F32),
        ],
        scratch_shapes=[
            pltpu.VMEM(((LRU_CONV - 1) * SUBLANES, LRU_WIDTH), F32),
            pltpu.VMEM((SUBLANES, LRU_WIDTH), F32),
            pltpu.VMEM((D_MODEL // LANES, STAGE_BLOCK_ROWS, LANES), F32),
        ],
        compiler_params=pltpu.CompilerParams(
            dimension_semantics=("arbitrary", "arbitrary"), vmem_limit_bytes=VMEM_LIMIT),
        name="mix_in",
    )(x, ng, win, cw, cb, wa, ba, wi, bi, lam, qg, kg, og)


def _moba_kernel(q_ref, k_ref, v_ref, km_ref, o_ref, vt_ref, ke_ref, s_ref, cm_ref):
    pair = pl.program_id(1)
    blk = MOBA_BLOCK
    nb = km_ref.shape[0]
    seq = k_ref.shape[0]
    blk_shift = blk.bit_length() - 1

    @pl.when((pl.program_id(0) == 0) & (pair == 0))
    def _():
        krow = lax.broadcasted_iota(jnp.int32, (seq, LANES), 0)
        klane = lax.broadcasted_iota(jnp.int32, (seq, LANES), 1)
        kblk = krow >> blk_shift
        extra = jnp.where(klane == kblk, 1.0, 0.0)
        extra = jnp.where(klane == nb, (kblk << blk_shift).astype(F32), extra)
        extra = jnp.where(klane == nb + 1, _time_offset(krow).astype(F32), extra)
        ke_ref[...] = extra.astype(BF16)
        causal = (_time_offset(lax.broadcasted_iota(jnp.int32, (blk, blk), 1))
                  >= _time_offset(lax.broadcasted_iota(jnp.int32, (blk, blk), 0)))
        cm_ref[...] = jnp.where(causal, 0.0, MASK_NEG)

    vt_ref[...] = v_ref[...].astype(F32).T.astype(BF16)

    kmb = km_ref[...].astype(BF16)
    lane_head = lax.broadcasted_iota(jnp.int32, (blk, PAIR_WIDTH), 1) >> 6
    blk_row = lax.broadcasted_iota(jnp.int32, (nb, blk), 0)
    feat_row = lax.broadcasted_iota(jnp.int32, (SUBLANES, blk), 0)
    zero_rows = jnp.zeros((LANES - nb - SUBLANES, blk), F32)
    dim_head = lax.broadcasted_iota(jnp.int32, (PAIR_WIDTH, blk), 0) >> 6

    units = [(qi, hh) for qi in range(seq // blk) for hh in range(2)]
    groups = blk // SUBLANES
    qa_of, mx_of, sum_of, acc_of, out_of = {}, {}, {}, {}, {}

    def prep(u):
        qi, hh = units[u]
        head = 2 * pair + hh
        slope = lax.bitcast_convert_type(
            jnp.broadcast_to((126 - head) << 23, (SUBLANES, blk)).astype(jnp.int32), F32)
        q = q_ref[qi * blk:(qi + 1) * blk, :]
        qh = jnp.where(lane_head == hh, q, jnp.zeros_like(q))
        gate = _dot_nt(kmb, qh)
        rank = jnp.zeros((nb, blk), F32)
        for m in range(qi):
            gm = gate[m:m + 1, :]
            rank = rank + jnp.where((gm > gate) | ((gm == gate) & (blk_row > m)), 1.0, 0.0)
        keep = (blk_row == qi) | ((blk_row < qi) & (rank < float(MOBA_TOPK)))
        feat_t = jnp.concatenate(
            [jnp.where(keep, 0.0, MASK_NEG), jnp.where(feat_row < 2, slope, 0.0), zero_rows], axis=0)
        qa_of[u] = jnp.concatenate([qh, feat_t.T.astype(BF16)], axis=1)

    def score_task(u, c):
        qi, _ = units[u]
        ka = jnp.concatenate([k_ref[c * blk:(c + 1) * blk, :], ke_ref[c * blk:(c + 1) * blk, :]], axis=1)
        s = _dot_nt(ka, qa_of[u])
        if c == qi:
            s = s + cm_ref[...]
        s_ref[u % 2, c * blk:(c + 1) * blk, :] = s
        part = jnp.max(s.reshape(groups, SUBLANES, blk), axis=0)
        mx_of[u] = part if c == 0 else jnp.maximum(mx_of[u], part)

    def prob_task(u, c):
        if c == 0:
            mx_of[u] = jnp.max(mx_of[u], axis=0, keepdims=True)
        p = jnp.exp(s_ref[u % 2, c * blk:(c + 1) * blk, :] - mx_of[u])
        part = jnp.sum(p.reshape(groups, SUBLANES, blk), axis=0)
        pv = _dot(vt_ref[:, c * blk:(c + 1) * blk], p.astype(BF16))
        sum_of[u] = part if c == 0 else sum_of[u] + part
        acc_of[u] = pv if c == 0 else acc_of[u] + pv

    def finish(u):
        qi, hh = units[u]
        out_of[hh] = acc_of.pop(u) / jnp.sum(sum_of.pop(u), axis=0, keepdims=True)
        if hh == 1:
            out_t = jnp.where(dim_head == 0, out_of[0], out_of[1])
            o_ref[qi * blk:(qi + 1) * blk, :] = out_t.T

    n_units = len(units)
    prep(0)
    prep(1)
    for c in range(units[0][0] + 1):
        score_task(0, c)
    for u in range(n_units):
        if u + 2 < n_units:
            prep(u + 2)
        n_prob = units[u][0] + 1
        n_score = units[u + 1][0] + 1 if u + 1 < n_units else 0
        for c in range(max(n_prob, n_score)):
            if c < n_score:
                score_task(u + 1, c)
            if c < n_prob:
                prob_task(u, c)
        finish(u)


def _moba(q, k, v, kmean):
    bsz, seq, _ = q.shape
    nb = seq // MOBA_BLOCK
    col = lambda rows: pl.BlockSpec((None, rows, PAIR_WIDTH), lambda b, p: (b, 0, p))
    return pl.pallas_call(
        _moba_kernel,
        grid=(bsz, N_PAIRS),
        in_specs=[col(seq), col(seq), col(seq), col(nb)],
        out_specs=col(seq),
        out_shape=jax.ShapeDtypeStruct((bsz, seq, ATTN_WIDTH), F32),
        scratch_shapes=[
            pltpu.VMEM((PAIR_WIDTH, seq), BF16),
            pltpu.VMEM((seq, LANES), BF16),
            pltpu.VMEM((2, seq, MOBA_BLOCK), F32),
            pltpu.VMEM((MOBA_BLOCK, MOBA_BLOCK), F32),
        ],
        compiler_params=pltpu.CompilerParams(
            dimension_semantics=("arbitrary", "arbitrary"), vmem_limit_bytes=VMEM_LIMIT),
        name="moba",
    )(q, k, v, kmean)


def _ffn_kernel(x_ref, yl_ref, ya_ref, ag_ref, wout_ref, fg_ref, wup_ref, cw_ref, cb_ref, wdn_ref,
                o_ref, hf_ref, up_ref, act_ref, stage_ref):
    t = pl.program_id(1)
    tm = TM_FFN
    halo = BF16_ROWS
    n_chunks = D_FF // TF

    @pl.when(t == 0)
    def _():
        hf_ref[0:halo, :] = jnp.zeros((halo, D_MODEL), BF16)

    blocks = tm // MOBA_BLOCK
    halo_groups = halo // SUBLANES
    assert halo_groups == FFN_CONV - 1
    row = lax.broadcasted_iota(jnp.int32, (SUBLANES, TF), 0)

    an = _rms(ya_ref[...], ag_ref[...]).astype(BF16)
    x1 = (_load_time_permuted(x_ref, stage_ref, blocks) + _dot(yl_ref[...], wout_ref[0:LRU_WIDTH, :])
          + _dot(an, wout_ref[LRU_WIDTH:LRU_WIDTH + ATTN_WIDTH, :]))
    hf = _rms(x1, fg_ref[...]).astype(BF16)
    hf_ref[halo:halo + tm, :] = hf

    def up_task(c):
        hfe = hf_ref[...]
        up_ref[c % 2, :, 0:TF] = _dot(hfe, wup_ref[:, c * TF:(c + 1) * TF])
        up_ref[c % 2, :, TF:2 * TF] = _dot(hfe, wup_ref[:, D_FF + c * TF:D_FF + (c + 1) * TF])

    def act_task(c):
        slot = c % 2

        def conv(col0, off):
            parts = []
            for b in range(blocks):
                base = halo + b * MOBA_BLOCK
                prefix = []
                for g in range(PERM_GROUPS - halo_groups, PERM_GROUPS):
                    r_now = base + g * SUBLANES
                    r_before = r_now - MOBA_BLOCK if b else (g - (PERM_GROUPS - halo_groups)) * SUBLANES
                    prefix.append(_prev_step_group(
                        up_ref[slot, r_before:r_before + SUBLANES, col0:col0 + TF],
                        up_ref[slot, r_now:r_now + SUBLANES, col0:col0 + TF], row))
                ext = jnp.concatenate(prefix + [up_ref[slot, base:base + MOBA_BLOCK, col0:col0 + TF]], axis=0)
                y = cb_ref[:, off:off + TF]
                for j in range(FFN_CONV):
                    y = y + cw_ref[j:j + 1, off:off + TF] * ext[j * SUBLANES:j * SUBLANES + MOBA_BLOCK, :]
                parts.append(y)
            return jnp.concatenate(parts, axis=0)

        act = jax.nn.gelu(conv(0, c * TF)) * conv(TF, D_FF + c * TF)
        act_ref[:, c * TF:(c + 1) * TF] = act.astype(BF16)

    up_task(0)
    for c in range(n_chunks):
        if c + 1 < n_chunks:
            up_task(c + 1)
        act_task(c)

    _store_time_unpermuted(o_ref, stage_ref, x1 + _dot(act_ref[...], wdn_ref[...]), blocks)
    hf_ref[0:halo, :] = hf[tm - halo:tm, :]


def _ffn(x, ylru, yattn, ag, wout, fg, wup, cw, cb, wdn):
    bsz, seq, _ = x.shape
    full = lambda shape: pl.BlockSpec(shape, lambda b, t: (0,) * len(shape), pipeline_mode=pl.Buffered(1))
    tile = lambda w: pl.BlockSpec((None, TM_FFN, w), lambda b, t: (b, t, 0))
    return pl.pallas_call(
        _ffn_kernel,
        grid=(bsz, seq // TM_FFN),
        in_specs=[
            tile(D_MODEL), tile(LRU_WIDTH), tile(ATTN_WIDTH),
            full((1, ATTN_WIDTH)),
            full((D_MODEL, D_MODEL)),
            full((1, D_MODEL)),
            full((D_MODEL, 2 * D_FF)),
            full((FFN_CONV, 2 * D_FF)),
            full((1, 2 * D_FF)),
            full((D_FF, D_MODEL)),
        ],
        out_specs=tile(D_MODEL),
        out_shape=jax.ShapeDtypeStruct((bsz, seq, D_MODEL), F32),
        scratch_shapes=[
            pltpu.VMEM((BF16_ROWS + TM_FFN, D_MODEL), BF16),
            pltpu.VMEM((2, BF16_ROWS + TM_FFN, 2 * TF), F32),
            pltpu.VMEM((TM_FFN, D_FF), BF16),
            pltpu.VMEM((D_MODEL // LANES, (TM_FFN // MOBA_BLOCK) * STAGE_BLOCK_ROWS, LANES), F32),
        ],
        compiler_params=pltpu.CompilerParams(
            dimension_semantics=("arbitrary", "arbitrary"), vmem_limit_bytes=VMEM_LIMIT),
        name="ffn",
    )(x, ylru, yattn, ag, wout, fg, wup, cw, cb, wdn)


def _block_diag_halves(w):
    hd = w.shape[-1]
    w4 = w.reshape(2, LRU_HEADS // 2, hd, hd)
    eye = jnp.eye(LRU_HEADS // 2, dtype=w.dtype)
    return jnp.einsum('ghij,hk->ghikj', w4, eye).reshape(2, LRU_WIDTH // 2, LRU_WIDTH // 2)


def kernel(x, mix_norm_g, w_in, lru_conv_w, lru_conv_b, lru_w_a, lru_b_a, lru_w_i, lru_b_i, lru_lambda,
           q_norm_g, k_norm_g, lru_out_g, attn_out_g, w_out, ffn_norm_g, w_up, ffn_conv_w, ffn_conv_b, w_down):
    depth = w_in.shape[0]
    bsz, seq, _ = x.shape
    assert seq % MOBA_BLOCK == 0 and seq // MOBA_BLOCK > MOBA_TOPK
    row = lambda p: p.reshape(1, -1)
    for layer in range(depth):
        ylru, q, k, v, kmean = _mix_in(
            x, row(mix_norm_g[layer]), w_in[layer].astype(BF16),
            lru_conv_w[layer], row(lru_conv_b[layer]),
            _block_diag_halves(lru_w_a[layer]).astype(BF16), row(lru_b_a[layer]),
            _block_diag_halves(lru_w_i[layer]).astype(BF16), row(lru_b_i[layer]),
            row(lru_lambda[layer]),
            row(jnp.tile(q_norm_g[layer], ATTN_HEADS)), row(jnp.tile(k_norm_g[layer], ATTN_HEADS)),
            row(lru_out_g[layer]))
        yattn = _moba(q, k, v, kmean.reshape(bsz, seq // MOBA_BLOCK, ATTN_WIDTH))
        x = _ffn(x, ylru, yattn, row(attn_out_g[layer]), w_out[layer].astype(BF16),
                 row(ffn_norm_g[layer]), w_up[layer].astype(BF16),
                 ffn_conv_w[layer], row(ffn_conv_b[layer]), w_down[layer].astype(BF16))
    return x
```

```python
import functools

import jax
import jax.numpy as jnp
from jax import lax
from jax.experimental import pallas as pl
from jax.experimental.pallas import tpu as pltpu

F32 = jnp.float32
BF16 = jnp.bfloat16

D_MODEL = 1024
LRU_WIDTH = 512
LRU_HEADS = 8
LRU_CONV = 4
LRU_C = 8.0
ATTN_WIDTH = 512
ATTN_HEADS = 8
HEAD_DIM = 64
IN_WIDTH = 2 * LRU_WIDTH + 3 * ATTN_WIDTH
MOBA_BLOCK = 256
MOBA_TOPK = 3
D_FF = 2816
FFN_CONV = 3
EPS = 1e-6

LANES = 128
SUBLANES = 8
BF16_ROWS = 16
PAIR_WIDTH = 2 * HEAD_DIM
N_PAIRS = ATTN_HEADS // 2
MASK_NEG = -1e30

TM_MIX = MOBA_BLOCK
TM_FFN = 512
TF = 256
VMEM_LIMIT = 56 * 1024 * 1024


def _rms(x, g):
    return x * lax.rsqrt(jnp.mean(x * x, axis=-1, keepdims=True) + EPS) * g


def _dot(a, b):
    return jnp.dot(a, b, preferred_element_type=F32)


PERM_GROUPS = MOBA_BLOCK // SUBLANES


STAGE_STRIDE = PERM_GROUPS + SUBLANES
STAGE_BLOCK_ROWS = SUBLANES * STAGE_STRIDE


def _load_time_permuted(ref, stage_ref, blocks):
    chunks = ref.shape[-1] // LANES
    for c in range(chunks):
        for b in range(blocks):
            for k in range(SUBLANES):
                src = b * MOBA_BLOCK + k * PERM_GROUPS
                dst = b * STAGE_BLOCK_ROWS + k * STAGE_STRIDE
                stage_ref[c, dst:dst + PERM_GROUPS, :] = ref[src:src + PERM_GROUPS, c * LANES:(c + 1) * LANES]
    return jnp.concatenate(
        [jnp.concatenate([stage_ref[c, pl.ds(b * STAGE_BLOCK_ROWS + g, SUBLANES, stride=STAGE_STRIDE), :]
                          for b in range(blocks) for g in range(PERM_GROUPS)], axis=0)
         for c in range(chunks)], axis=1)


def _store_time_unpermuted(ref, stage_ref, val, blocks):
    chunks = ref.shape[-1] // LANES
    for c in range(chunks):
        for b in range(blocks):
            for g in range(PERM_GROUPS):
                r0 = b * MOBA_BLOCK + g * SUBLANES
                stage_ref[c, pl.ds(b * STAGE_BLOCK_ROWS + g, SUBLANES, stride=STAGE_STRIDE), :] = (
                    val[r0:r0 + SUBLANES, c * LANES:(c + 1) * LANES])
    for c in range(chunks):
        for b in range(blocks):
            for k in range(SUBLANES):
                dst = b * MOBA_BLOCK + k * PERM_GROUPS
                src = b * STAGE_BLOCK_ROWS + k * STAGE_STRIDE
                ref[dst:dst + PERM_GROUPS, c * LANES:(c + 1) * LANES] = stage_ref[c, src:src + PERM_GROUPS, :]


def _time_offset(row):
    group = (row & (MOBA_BLOCK - 1)) >> (SUBLANES.bit_length() - 1)
    return group + (row & (SUBLANES - 1)) * PERM_GROUPS


def _prev_step_group(before, group, row):
    return jnp.where(row == 0, pltpu.roll(before, 1, 0), pltpu.roll(group, 1, 0))


def _dot_nt(a, b):
    return lax.dot_general(a, b, (((1,), (1,)), ((), ())), preferred_element_type=F32)


def _mix_in_kernel(x_ref, ng_ref, win_ref, cw_ref, cb_ref, wa_ref, ba_ref, wi_ref, bi_ref,
                   lam_ref, qg_ref, kg_ref, og_ref,
                   ylru_ref, q_ref, k_ref, v_ref, km_ref,
                   xe_ref, hc_ref, stage_ref):
    t = pl.program_id(1)
    tm = TM_MIX
    tail = (LRU_CONV - 1) * SUBLANES

    @pl.when(t == 0)
    def _():
        xe_ref[...] = jnp.zeros((tail, LRU_WIDTH), F32)
        hc_ref[...] = jnp.zeros((SUBLANES, LRU_WIDTH), F32)

    h = _rms(_load_time_permuted(x_ref, stage_ref, 1), ng_ref[...]).astype(BF16)
    row = lax.broadcasted_iota(jnp.int32, (SUBLANES, LRU_WIDTH), 0)

    xr = _dot(h, win_ref[:, 0:LRU_WIDTH])
    base = 2 * LRU_WIDTH
    v_ref[...] = _dot(h, win_ref[:, base + 2 * ATTN_WIDTH:base + 3 * ATTN_WIDTH]).astype(BF16)
    prefix = [_prev_step_group(xe_ref[g * SUBLANES:(g + 1) * SUBLANES, :],
                               xr[tm - tail + g * SUBLANES:tm - tail + (g + 1) * SUBLANES, :], row)
              for g in range(LRU_CONV - 1)]
    ext = jnp.concatenate(prefix + [xr], axis=0)
    xc = cb_ref[...]
    for j in range(LRU_CONV):
        xc = xc + cw_ref[j:j + 1, :] * ext[j * SUBLANES:j * SUBLANES + tm, :]
    xe_ref[...] = xr[tm - tail:tm, :]

    xcb = xc.astype(BF16)
    half = LRU_WIDTH // 2
    r = jnp.concatenate([_dot(xcb[:, g * half:(g + 1) * half], wa_ref[g]) for g in range(2)], axis=-1)
    i = jnp.concatenate([_dot(xcb[:, g * half:(g + 1) * half], wi_ref[g]) for g in range(2)], axis=-1)
    q = _dot(h, win_ref[:, base:base + ATTN_WIDTH])
    r = jax.nn.sigmoid(r + ba_ref[...])
    i = jax.nn.sigmoid(i + bi_ref[...])
    nlam = -lam_ref[...]
    softplus = jnp.maximum(nlam, 0.0) + jnp.log1p(jnp.exp(-jnp.abs(nlam)))
    log_a = (-LRU_C * r) * softplus
    a = jnp.exp(log_a)
    one_m_a2 = 1.0 - a * a
    root = jnp.where(one_m_a2 > 0.0, one_m_a2 * lax.rsqrt(one_m_a2), 0.0)
    u = root * (i * xc)
    k = _dot(h, win_ref[:, base + ATTN_WIDTH:base + 2 * ATTN_WIDTH])

    groups = tm // SUBLANES
    h_loc = [u[0:SUBLANES, :]]
    a_cum = [a[0:SUBLANES, :]]
    for g in range(1, groups):
        ag = a[g * SUBLANES:(g + 1) * SUBLANES, :]
        h_loc.append(ag * h_loc[-1] + u[g * SUBLANES:(g + 1) * SUBLANES, :])
        a_cum.append(ag * a_cum[-1])
    end_h, end_a = h_loc[-1], a_cum[-1]
    for d in (1, 2, 4):
        keep = row >= d
        a_sh = jnp.where(keep, pltpu.roll(end_a, d, 0), 1.0)
        h_sh = jnp.where(keep, pltpu.roll(end_h, d, 0), 0.0)
        end_h = end_h + end_a * h_sh
        end_a = end_a * a_sh
    carry = hc_ref[SUBLANES - 1:SUBLANES, :]
    chain_end = end_h + end_a * carry
    chain_start = jnp.where(row == 0, carry, pltpu.roll(chain_end, 1, 0))
    hc_ref[...] = chain_end
    yg = _dot(h, win_ref[:, LRU_WIDTH:2 * LRU_WIDTH])
    hs = jnp.concatenate([h_loc[g] + a_cum[g] * chain_start for g in range(groups)], axis=0)

    ylru_ref[...] = _rms(hs * jax.nn.gelu(yg), og_ref[...]).astype(BF16)

    ri = lax.broadcasted_iota(jnp.int32, (LANES, LANES), 0) >> 6
    ci = lax.broadcasted_iota(jnp.int32, (LANES, LANES), 1) >> 6
    head_ones = jnp.where(ri == ci, 1.0, 0.0).astype(BF16)

    def head_norm(z, g):
        sq = (z * z).astype(BF16)
        ssq = jnp.concatenate(
            [_dot(sq[:, c * LANES:(c + 1) * LANES], head_ones) for c in range(ATTN_WIDTH // LANES)], axis=-1)
        return z * lax.rsqrt(ssq * (1.0 / HEAD_DIM) + EPS) * g

    q_ref[...] = (head_norm(q, qg_ref[...]) * (HEAD_DIM ** -0.5)).astype(BF16)
    kn = head_norm(k, kg_ref[...])
    k_ref[...] = kn.astype(BF16)
    km_ref[...] = jnp.mean(kn, axis=0, keepdims=True)


def _mix_in(x, ng, win, cw, cb, wa, ba, wi, bi, lam, qg, kg, og):
    bsz, seq, _ = x.shape
    nt = seq // TM_MIX
    full = lambda shape: pl.BlockSpec(shape, lambda b, t: (0,) * len(shape))
    tile = lambda w: pl.BlockSpec((None, TM_MIX, w), lambda b, t: (b, t, 0))
    return pl.pallas_call(
        _mix_in_kernel,
        grid=(bsz, nt),
        in_specs=[
            tile(D_MODEL),
            full((1, D_MODEL)),
            full((D_MODEL, IN_WIDTH)),
            full((LRU_CONV, LRU_WIDTH)),
            full((1, LRU_WIDTH)),
            full((2, LRU_WIDTH // 2, LRU_WIDTH // 2)),
            full((1, LRU_WIDTH)),
            full((2, LRU_WIDTH // 2, LRU_WIDTH // 2)),
            full((1, LRU_WIDTH)),
            full((1, LRU_WIDTH)),
            full((1, ATTN_WIDTH)),
            full((1, ATTN_WIDTH)),
            full((1, LRU_WIDTH)),
        ],
        out_specs=[
            tile(LRU_WIDTH), tile(ATTN_WIDTH), tile(ATTN_WIDTH), tile(ATTN_WIDTH),
            pl.BlockSpec((None, None, 1, ATTN_WIDTH), lambda b, t: (b, t, 0, 0)),
        ],
        out_shape=[
            jax.ShapeDtypeStruct((bsz, seq, LRU_WIDTH), BF16),
            jax.ShapeDtypeStruct((bsz, seq, ATTN_WIDTH), BF16),
            jax.ShapeDtypeStruct((bsz, seq, ATTN_WIDTH), BF16),
            jax.ShapeDtypeStruct((bsz, seq, ATTN_WIDTH), BF16),
            jax.ShapeDtypeStruct((bsz, nt, 1, ATTN_WIDTH), F32),
        ],
        scratch_shapes=[
            pltpu.VMEM(((LRU_CONV - 1) * SUBLANES, LRU_WIDTH), F32),
            pltpu.VMEM((SUBLANES, LRU_WIDTH), F32),
            pltpu.VMEM((D_MODEL // LANES, STAGE_BLOCK_ROWS, LANES), F32),
        ],
        compiler_params=pltpu.CompilerParams(
            dimension_semantics=("arbitrary", "arbitrary"), vmem_limit_bytes=VMEM_LIMIT),
        name="mix_in",
    )(x, ng, win, cw, cb, wa, ba, wi, bi, lam, qg, kg, og)


def _moba_kernel(q_ref, k_ref, v_ref, km_ref, o_ref, vt_ref, ke_ref, s_ref, p_ref, cm_ref):
    pair = pl.program_id(1)
    blk = MOBA_BLOCK
    nb = km_ref.shape[0]
    seq = k_ref.shape[0]
    blk_shift = blk.bit_length() - 1

    @pl.when((pl.program_id(0) == 0) & (pair == 0))
    def _():
        krow = lax.broadcasted_iota(jnp.int32, (seq, LANES), 0)
        klane = lax.broadcasted_iota(jnp.int32, (seq, LANES), 1)
        kblk = krow >> blk_shift
        extra = jnp.where(klane == kblk, 1.0, 0.0)
        extra = jnp.where(klane == nb, (kblk << blk_shift).astype(F32), extra)
        extra = jnp.where(klane == nb + 1, _time_offset(krow).astype(F32), extra)
        ke_ref[...] = extra.astype(BF16)
        causal = (_time_offset(lax.broadcasted_iota(jnp.int32, (blk, blk), 1))
                  >= _time_offset(lax.broadcasted_iota(jnp.int32, (blk, blk), 0)))
        cm_ref[...] = jnp.where(causal, 0.0, MASK_NEG)

    vt_ref[...] = v_ref[...].astype(F32).T.astype(BF16)

    kmb = km_ref[...].astype(BF16)
    lane_head = lax.broadcasted_iota(jnp.int32, (blk, PAIR_WIDTH), 1) >> 6
    blk_row = lax.broadcasted_iota(jnp.int32, (nb, blk), 0)
    feat_row = lax.broadcasted_iota(jnp.int32, (SUBLANES, blk), 0)
    zero_rows = jnp.zeros((LANES - nb - SUBLANES, blk), F32)
    dim_head = lax.broadcasted_iota(jnp.int32, (PAIR_WIDTH, blk), 0) >> 6

    units = [(qi, hh) for qi in range(seq // blk) for hh in range(2)]
    groups = blk // SUBLANES
    qa_of, mx_of, sum_of, out_of = {}, {}, {}, {}

    def prep(u):
        qi, hh = units[u]
        head = 2 * pair + hh
        slope = lax.bitcast_convert_type(
            jnp.broadcast_to((126 - head) << 23, (SUBLANES, blk)).astype(jnp.int32), F32)
        q = q_ref[qi * blk:(qi + 1) * blk, :]
        qh = jnp.where(lane_head == hh, q, jnp.zeros_like(q))
        gate = _dot_nt(kmb, qh)
        rank = jnp.zeros((nb, blk), F32)
        for m in range(qi):
            gm = gate[m:m + 1, :]
            rank = rank + jnp.where((gm > gate) | ((gm == gate) & (blk_row > m)), 1.0, 0.0)
        keep = (blk_row == qi) | ((blk_row < qi) & (rank < float(MOBA_TOPK)))
        slope_rows = jnp.where(feat_row < 2, slope, 0.0)
        feat_t = jnp.concatenate([jnp.where(keep, 0.0, MASK_NEG), slope_rows, zero_rows], axis=0)
        qa_of[u] = jnp.concatenate([qh.astype(F32).T, feat_t], axis=0).astype(BF16)

    def score_dot(u):
        rows = (units[u][0] + 1) * blk
        ka = jnp.concatenate([k_ref[0:rows, :], ke_ref[0:rows, :]], axis=1)
        return _dot(ka, qa_of.pop(u))

    def score_chunk(u, c, s_all):
        s = s_all[c * blk:(c + 1) * blk, :]
        if c == units[u][0]:
            s = s + cm_ref[...]
        s_ref[u % 2, c * blk:(c + 1) * blk, :] = s
        part = jnp.max(s.reshape(groups, SUBLANES, blk), axis=0)
        mx_of[u] = part if c == 0 else jnp.maximum(mx_of[u], part)

    def prob_chunk(u, c, mx):
        p = jnp.exp(s_ref[u % 2, c * blk:(c + 1) * blk, :] - mx)
        p_ref[u % 2, c * blk:(c + 1) * blk, :] = p.astype(BF16)
        part = jnp.sum(p.reshape(groups, SUBLANES, blk), axis=0)
        sum_of[u] = part if c == 0 else sum_of[u] + part

    def value_stage(u):
        qi, hh = units[u]
        rows = (qi + 1) * blk
        acc = _dot(vt_ref[:, 0:rows], p_ref[u % 2, 0:rows, :])
        out_of[hh] = acc / jnp.sum(sum_of.pop(u), axis=0, keepdims=True)
        if hh == 1:
            out_t = jnp.where(dim_head == 0, out_of[0], out_of[1])
            o_ref[qi * blk:(qi + 1) * blk, :] = out_t.T

    n_units = len(units)
    for step in range(n_units + 3):
        if step < n_units:
            prep(step)
        u_score = step - 1 if 1 <= step < n_units + 1 else None
        u_prob = step - 2 if 2 <= step < n_units + 2 else None
        n_score = units[u_score][0] + 1 if u_score is not None else 0
        n_prob = units[u_prob][0] + 1 if u_prob is not None else 0
        if u_prob is not None:
            mx = jnp.max(mx_of.pop(u_prob), axis=0, keepdims=True)
        if u_score is not None:
            s_all = score_dot(u_score)
        for c in range(max(n_score, n_prob)):
            if c < n_prob:
                prob_chunk(u_prob, c, mx)
            if c < n_score:
                score_chunk(u_score, c, s_all)
        if 3 <= step:
            value_stage(step - 3)


def _moba(q, k, v, kmean):
    bsz, seq, _ = q.shape
    nb = seq // MOBA_BLOCK
    col = lambda rows: pl.BlockSpec((None, rows, PAIR_WIDTH), lambda b, p: (b, 0, p))
    return pl.pallas_call(
        _moba_kernel,
        grid=(bsz, N_PAIRS),
        in_specs=[col(seq), col(seq), col(seq), col(nb)],
        out_specs=col(seq),
        out_shape=jax.ShapeDtypeStruct((bsz, seq, ATTN_WIDTH), F32),
        scratch_shapes=[
            pltpu.VMEM((PAIR_WIDTH, seq), BF16),
            pltpu.VMEM((seq, LANES), BF16),
            pltpu.VMEM((2, seq, MOBA_BLOCK), F32),
            pltpu.VMEM((2, seq, MOBA_BLOCK), BF16),
            pltpu.VMEM((MOBA_BLOCK, MOBA_BLOCK), F32),
        ],
        compiler_params=pltpu.CompilerParams(
            dimension_semantics=("arbitrary", "arbitrary"), vmem_limit_bytes=VMEM_LIMIT),
        name="moba",
    )(q, k, v, kmean)


def _ffn_kernel(x_ref, yl_ref, ya_ref, ag_ref, wout_ref, fg_ref, wup_ref, cw_ref, cb_ref, wdn_ref,
                o_ref, up_ref, tail_ref, act_ref, stage_ref):
    t = pl.program_id(1)
    tm = TM_FFN
    n_chunks = D_FF // TF
    blocks = tm // MOBA_BLOCK
    tail_groups = FFN_CONV - 1
    tail = tail_groups * SUBLANES
    row = lax.broadcasted_iota(jnp.int32, (SUBLANES, TF), 0)

    @pl.when(t == 0)
    def _():
        tail_ref[...] = jnp.zeros(tail_ref.shape, F32)

    an = _rms(ya_ref[...], ag_ref[...]).astype(BF16)
    x1 = (_load_time_permuted(x_ref, stage_ref, blocks) + _dot(yl_ref[...], wout_ref[0:LRU_WIDTH, :])
          + _dot(an, wout_ref[LRU_WIDTH:LRU_WIDTH + ATTN_WIDTH, :]))
    hf = _rms(x1, fg_ref[...]).astype(BF16)

    def up_task(c):
        up_ref[c % 2, :, 0:TF] = _dot(hf, wup_ref[:, c * TF:(c + 1) * TF])
        up_ref[c % 2, :, TF:2 * TF] = _dot(hf, wup_ref[:, D_FF + c * TF:D_FF + (c + 1) * TF])

    def act_task(c):
        slot = c % 2

        def conv(col0, off):
            parts = []
            for b in range(blocks):
                base = b * MOBA_BLOCK
                prefix = []
                for i in range(tail_groups):
                    r_now = base + MOBA_BLOCK - tail + i * SUBLANES
                    if b:
                        before = up_ref[slot, r_now - MOBA_BLOCK:r_now - MOBA_BLOCK + SUBLANES, col0:col0 + TF]
                    else:
                        before = tail_ref[c, i * SUBLANES:(i + 1) * SUBLANES, col0:col0 + TF]
                    prefix.append(_prev_step_group(
                        before, up_ref[slot, r_now:r_now + SUBLANES, col0:col0 + TF], row))
                ext = jnp.concatenate(prefix + [up_ref[slot, base:base + MOBA_BLOCK, col0:col0 + TF]], axis=0)
                y = cb_ref[:, off:off + TF]
                for j in range(FFN_CONV):
                    y = y + cw_ref[j:j + 1, off:off + TF] * ext[j * SUBLANES:j * SUBLANES + MOBA_BLOCK, :]
                parts.append(y)
            return jnp.concatenate(parts, axis=0)

        act = jax.nn.gelu(conv(0, c * TF)) * conv(TF, D_FF + c * TF)
        act_ref[:, c * TF:(c + 1) * TF] = act.astype(BF16)
        tail_ref[c] = up_ref[slot, tm - tail:tm, :]

    up_task(0)
    for c in range(n_chunks):
        if c + 1 < n_chunks:
            up_task(c + 1)
        act_task(c)

    _store_time_unpermuted(o_ref, stage_ref, x1 + _dot(act_ref[...], wdn_ref[...]), blocks)


def _ffn(x, ylru, yattn, ag, wout, fg, wup, cw, cb, wdn):
    bsz, seq, _ = x.shape
    full = lambda shape: pl.BlockSpec(shape, lambda b, t: (0,) * len(shape), pipeline_mode=pl.Buffered(1))
    tile = lambda w: pl.BlockSpec((None, TM_FFN, w), lambda b, t: (b, t, 0))
    return pl.pallas_call(
        _ffn_kernel,
        grid=(bsz, seq // TM_FFN),
        in_specs=[
            tile(D_MODEL), tile(LRU_WIDTH), tile(ATTN_WIDTH),
            full((1, ATTN_WIDTH)),
            full((D_MODEL, D_MODEL)),
            full((1, D_MODEL)),
            full((D_MODEL, 2 * D_FF)),
            full((FFN_CONV, 2 * D_FF)),
            full((1, 2 * D_FF)),
            full((D_FF, D_MODEL)),
        ],
        out_specs=tile(D_MODEL),
        out_shape=jax.ShapeDtypeStruct((bsz, seq, D_MODEL), F32),
        scratch_shapes=[
            pltpu.VMEM((2, TM_FFN, 2 * TF), F32),
            pltpu.VMEM((D_FF // TF, (FFN_CONV - 1) * SUBLANES, 2 * TF), F32),
            pltpu.VMEM((TM_FFN, D_FF), BF16),
            pltpu.VMEM((D_MODEL // LANES, (TM_FFN // MOBA_BLOCK) * STAGE_BLOCK_ROWS, LANES), F32),
        ],
        compiler_params=pltpu.CompilerParams(
            dimension_semantics=("arbitrary", "arbitrary"), vmem_limit_bytes=VMEM_LIMIT),
        name="ffn",
    )(x, ylru, yattn, ag, wout, fg, wup, cw, cb, wdn)


def _block_diag_halves(w):
    hd = w.shape[-1]
    w4 = w.reshape(2, LRU_HEADS // 2, hd, hd)
    eye = jnp.eye(LRU_HEADS // 2, dtype=w.dtype)
    return jnp.einsum('ghij,hk->ghikj', w4, eye).reshape(2, LRU_WIDTH // 2, LRU_WIDTH // 2)


def kernel(x, mix_norm_g, w_in, lru_conv_w, lru_conv_b, lru_w_a, lru_b_a, lru_w_i, lru_b_i, lru_lambda,
           q_norm_g, k_norm_g, lru_out_g, attn_out_g, w_out, ffn_norm_g, w_up, ffn_conv_w, ffn_conv_b, w_down):
    depth = w_in.shape[0]
    bsz, seq, _ = x.shape
    assert seq % MOBA_BLOCK == 0 and seq // MOBA_BLOCK > MOBA_TOPK
    row = lambda p: p.reshape(1, -1)
    for layer in range(depth):
        ylru, q, k, v, kmean = _mix_in(
            x, row(mix_norm_g[layer]), w_in[layer].astype(BF16),
            lru_conv_w[layer], row(lru_conv_b[layer]),
            _block_diag_halves(lru_w_a[layer]).astype(BF16), row(lru_b_a[layer]),
            _block_diag_halves(lru_w_i[layer]).astype(BF16), row(lru_b_i[layer]),
            row(lru_lambda[layer]),
            row(jnp.tile(q_norm_g[layer], ATTN_HEADS)), row(jnp.tile(k_norm_g[layer], ATTN_HEADS)),
            row(lru_out_g[layer]))
        yattn = _moba(q, k, v, kmean.reshape(bsz, seq // MOBA_BLOCK, ATTN_WIDTH))
        x = _ffn(x, ylru, yattn, row(attn_out_g[layer]), w_out[layer].astype(BF16),
                 row(ffn_norm_g[layer]), w_up[layer].astype(BF16),
                 ffn_conv_w[layer], row(ffn_conv_b[layer]), w_down[layer].astype(BF16))
    return x
```

```python
import functools

import jax
import jax.numpy as jnp
from jax import lax
from jax.experimental import pallas as pl
from jax.experimental.pallas import tpu as pltpu

F32 = jnp.float32
BF16 = jnp.bfloat16

D_MODEL = 1024
LRU_WIDTH = 512
LRU_HEADS = 8
LRU_CONV = 4
LRU_C = 8.0
ATTN_WIDTH = 512
ATTN_HEADS = 8
HEAD_DIM = 64
IN_WIDTH = 2 * LRU_WIDTH + 3 * ATTN_WIDTH
MOBA_BLOCK = 256
MOBA_TOPK = 3
D_FF = 2816
FFN_CONV = 3
EPS = 1e-6

LANES = 128
SUBLANES = 8
BF16_ROWS = 16
PAIR_WIDTH = 2 * HEAD_DIM
N_PAIRS = ATTN_HEADS // 2
MASK_NEG = -1e30

TM_MIX = 4 * MOBA_BLOCK
PROJ_SLOTS = 2
TM_FFN = 512
TF = 256
VMEM_LIMIT = 56 * 1024 * 1024


def _rms(x, g):
    return x * lax.rsqrt(jnp.mean(x * x, axis=-1, keepdims=True) + EPS) * g


def _dot(a, b):
    return jnp.dot(a, b, preferred_element_type=F32)


PERM_GROUPS = MOBA_BLOCK // SUBLANES


STAGE_STRIDE = PERM_GROUPS + SUBLANES
STAGE_BLOCK_ROWS = SUBLANES * STAGE_STRIDE


def _load_time_permuted(ref, stage_ref, blocks):
    chunks = ref.shape[-1] // LANES
    for c in range(chunks):
        for b in range(blocks):
            for k in range(SUBLANES):
                src = b * MOBA_BLOCK + k * PERM_GROUPS
                dst = b * STAGE_BLOCK_ROWS + k * STAGE_STRIDE
                stage_ref[c, dst:dst + PERM_GROUPS, :] = ref[src:src + PERM_GROUPS, c * LANES:(c + 1) * LANES]
    return jnp.concatenate(
        [jnp.concatenate([stage_ref[c, pl.ds(b * STAGE_BLOCK_ROWS + g, SUBLANES, stride=STAGE_STRIDE), :]
                          for b in range(blocks) for g in range(PERM_GROUPS)], axis=0)
         for c in range(chunks)], axis=1)


def _store_time_unpermuted(ref, stage_ref, val, blocks):
    chunks = ref.shape[-1] // LANES
    for c in range(chunks):
        for b in range(blocks):
            for g in range(PERM_GROUPS):
                r0 = b * MOBA_BLOCK + g * SUBLANES
                stage_ref[c, pl.ds(b * STAGE_BLOCK_ROWS + g, SUBLANES, stride=STAGE_STRIDE), :] = (
                    val[r0:r0 + SUBLANES, c * LANES:(c + 1) * LANES])
    for c in range(chunks):
        for b in range(blocks):
            for k in range(SUBLANES):
                dst = b * MOBA_BLOCK + k * PERM_GROUPS
                src = b * STAGE_BLOCK_ROWS + k * STAGE_STRIDE
                ref[dst:dst + PERM_GROUPS, c * LANES:(c + 1) * LANES] = stage_ref[c, src:src + PERM_GROUPS, :]


def _time_offset(row):
    group = (row & (MOBA_BLOCK - 1)) >> (SUBLANES.bit_length() - 1)
    return group + (row & (SUBLANES - 1)) * PERM_GROUPS


def _prev_step_group(before, group, row):
    return jnp.where(row == 0, pltpu.roll(before, 1, 0), pltpu.roll(group, 1, 0))


def _dot_nt(a, b):
    return lax.dot_general(a, b, (((1,), (1,)), ((), ())), preferred_element_type=F32)


def _mix_in_kernel(x_ref, ng_ref, win_ref, cw_ref, cb_ref, wa_ref, ba_ref, wi_ref, bi_ref,
                   lam_ref, qg_ref, kg_ref, og_ref,
                   ylru_ref, q_ref, k_ref, v_ref, km_ref,
                   xe_ref, hc_ref, stage_ref, proj_ref):
    t = pl.program_id(1)
    blk = MOBA_BLOCK
    n_blocks = TM_MIX // blk
    tail = (LRU_CONV - 1) * SUBLANES
    groups = blk // SUBLANES

    @pl.when(t == 0)
    def _():
        xe_ref[...] = jnp.zeros((tail, LRU_WIDTH), F32)
        hc_ref[...] = jnp.zeros((SUBLANES, LRU_WIDTH), F32)

    x_perm = _load_time_permuted(x_ref, stage_ref, n_blocks)
    row = lax.broadcasted_iota(jnp.int32, (SUBLANES, LRU_WIDTH), 0)
    nlam = -lam_ref[...]
    softplus = jnp.maximum(nlam, 0.0) + jnp.log1p(jnp.exp(-jnp.abs(nlam)))
    ri = lax.broadcasted_iota(jnp.int32, (LANES, LANES), 0) >> 6
    ci = lax.broadcasted_iota(jnp.int32, (LANES, LANES), 1) >> 6
    head_ones = jnp.where(ri == ci, 1.0, 0.0).astype(BF16)
    base = 2 * LRU_WIDTH
    half = LRU_WIDTH // 2

    def head_norm(z, g):
        sq = (z * z).astype(BF16)
        ssq = jnp.concatenate(
            [_dot(sq[:, c * LANES:(c + 1) * LANES], head_ones) for c in range(ATTN_WIDTH // LANES)], axis=-1)
        return z * lax.rsqrt(ssq * (1.0 / HEAD_DIM) + EPS) * g

    def project_steps(b):
        rows = slice(b * blk, (b + 1) * blk)
        env = {}

        def norm():
            env['h'] = _rms(x_perm[rows, :], ng_ref[...]).astype(BF16)

        def piece(lo, hi):
            def run():
                proj_ref[b % PROJ_SLOTS, :, lo:hi] = _dot(env['h'], win_ref[:, lo:hi])
            return run

        def first():
            norm()
            piece(0, LRU_WIDTH)()

        def values():
            v_ref[rows, :] = _dot(env['h'], win_ref[:, base + 2 * ATTN_WIDTH:base + 3 * ATTN_WIDTH]).astype(BF16)

        return [first, piece(base, base + ATTN_WIDTH), piece(base + ATTN_WIDTH, base + 2 * ATTN_WIDTH),
                piece(LRU_WIDTH, base), values]

    def mix_steps(b, carried):
        rows = slice(b * blk, (b + 1) * blk)
        slot = b % PROJ_SLOTS
        env = {}

        def conv_and_gates():
            xr = proj_ref[slot, :, 0:LRU_WIDTH]
            prefix = [_prev_step_group(carried['xr'][g * SUBLANES:(g + 1) * SUBLANES, :],
                                       xr[blk - tail + g * SUBLANES:blk - tail + (g + 1) * SUBLANES, :], row)
                      for g in range(LRU_CONV - 1)]
            ext = jnp.concatenate(prefix + [xr], axis=0)
            xc = cb_ref[...]
            for j in range(LRU_CONV):
                xc = xc + cw_ref[j:j + 1, :] * ext[j * SUBLANES:j * SUBLANES + blk, :]
            carried['xr'] = xr[blk - tail:blk, :]
            xcb = xc.astype(BF16)
            env['xc'] = xc
            env['r'] = jnp.concatenate(
                [_dot(xcb[:, g * half:(g + 1) * half], wa_ref[g]) for g in range(2)], axis=-1)
            env['i'] = jnp.concatenate(
                [_dot(xcb[:, g * half:(g + 1) * half], wi_ref[g]) for g in range(2)], axis=-1)

        def decay_and_input():
            r = jax.nn.sigmoid(env.pop('r') + ba_ref[...])
            i = jax.nn.sigmoid(env.pop('i') + bi_ref[...])
            log_a = (-LRU_C * r) * softplus
            a = jnp.exp(log_a)
            one_m_a2 = 1.0 - a * a
            root = jnp.where(one_m_a2 > 0.0, one_m_a2 * lax.rsqrt(one_m_a2), 0.0)
            env['a'] = a
            env['u'] = root * (i * env.pop('xc'))

        def recurrence():
            a, u = env.pop('a'), env.pop('u')
            h_loc = [u[0:SUBLANES, :]]
            a_cum = [a[0:SUBLANES, :]]
            for g in range(1, groups):
                ag = a[g * SUBLANES:(g + 1) * SUBLANES, :]
                h_loc.append(ag * h_loc[-1] + u[g * SUBLANES:(g + 1) * SUBLANES, :])
                a_cum.append(ag * a_cum[-1])
            end_h, end_a = h_loc[-1], a_cum[-1]
            for d in (1, 2, 4):
                keep = row >= d
                a_sh = jnp.where(keep, pltpu.roll(end_a, d, 0), 1.0)
                h_sh = jnp.where(keep, pltpu.roll(end_h, d, 0), 0.0)
                end_h = end_h + end_a * h_sh
                end_a = end_a * a_sh
            carry = carried['h'][SUBLANES - 1:SUBLANES, :]
            chain_end = end_h + end_a * carry
            chain_start = jnp.where(row == 0, carry, pltpu.roll(chain_end, 1, 0))
            carried['h'] = chain_end
            env['hs'] = jnp.concatenate([h_loc[g] + a_cum[g] * chain_start for g in range(groups)], axis=0)

        def lru_out():
            yg = proj_ref[slot, :, LRU_WIDTH:base]
            ylru_ref[rows, :] = _rms(env.pop('hs') * jax.nn.gelu(yg), og_ref[...]).astype(BF16)

        def qk_out():
            q = proj_ref[slot, :, base:base + ATTN_WIDTH]
            k = proj_ref[slot, :, base + ATTN_WIDTH:base + 2 * ATTN_WIDTH]
            q_ref[rows, :] = (head_norm(q, qg_ref[...]) * (HEAD_DIM ** -0.5)).astype(BF16)
            kn = head_norm(k, kg_ref[...])
            k_ref[rows, :] = kn.astype(BF16)
            km_ref[b] = jnp.mean(kn, axis=0, keepdims=True)

        return [conv_and_gates, decay_and_input, recurrence, lru_out, qk_out]

    carried = {'xr': xe_ref[...], 'h': hc_ref[...]}
    for step in project_steps(0):
        step()
    for b in range(n_blocks):
        ahead = project_steps(b + 1) if b + 1 < n_blocks else []
        for i, step in enumerate(mix_steps(b, carried)):
            if i < len(ahead):
                ahead[i]()
            step()
    xe_ref[...] = carried['xr']
    hc_ref[...] = carried['h']


def _mix_in(x, ng, win, cw, cb, wa, ba, wi, bi, lam, qg, kg, og):
    bsz, seq, _ = x.shape
    nt = seq // TM_MIX
    full = lambda shape: pl.BlockSpec(shape, lambda b, t: (0,) * len(shape))
    tile = lambda w: pl.BlockSpec((None, TM_MIX, w), lambda b, t: (b, t, 0))
    return pl.pallas_call(
        _mix_in_kernel,
        grid=(bsz, nt),
        in_specs=[
            tile(D_MODEL),
            full((1, D_MODEL)),
            full((D_MODEL, IN_WIDTH)),
            full((LRU_CONV, LRU_WIDTH)),
            full((1, LRU_WIDTH)),
            full((2, LRU_WIDTH // 2, LRU_WIDTH // 2)),
            full((1, LRU_WIDTH)),
            full((2, LRU_WIDTH // 2, LRU_WIDTH // 2)),
            full((1, LRU_WIDTH)),
            full((1, LRU_WIDTH)),
            full((1, ATTN_WIDTH)),
            full((1, ATTN_WIDTH)),
            full((1, LRU_WIDTH)),
        ],
        out_specs=[
            tile(LRU_WIDTH), tile(ATTN_WIDTH), tile(ATTN_WIDTH), tile(ATTN_WIDTH),
            pl.BlockSpec((None, TM_MIX // MOBA_BLOCK, 1, ATTN_WIDTH), lambda b, t: (b, t, 0, 0)),
        ],
        out_shape=[
            jax.ShapeDtypeStruct((bsz, seq, LRU_WIDTH), BF16),
            jax.ShapeDtypeStruct((bsz, seq, ATTN_WIDTH), BF16),
            jax.ShapeDtypeStruct((bsz, seq, ATTN_WIDTH), BF16),
            jax.ShapeDtypeStruct((bsz, seq, ATTN_WIDTH), BF16),
            jax.ShapeDtypeStruct((bsz, seq // MOBA_BLOCK, 1, ATTN_WIDTH), F32),
        ],
        scratch_shapes=[
            pltpu.VMEM(((LRU_CONV - 1) * SUBLANES, LRU_WIDTH), F32),
            pltpu.VMEM((SUBLANES, LRU_WIDTH), F32),
            pltpu.VMEM((D_MODEL // LANES, (TM_MIX // MOBA_BLOCK) * STAGE_BLOCK_ROWS, LANES), F32),
            pltpu.VMEM((PROJ_SLOTS, MOBA_BLOCK, 2 * LRU_WIDTH + 2 * ATTN_WIDTH), F32),
        ],
        compiler_params=pltpu.CompilerParams(
            dimension_semantics=("arbitrary", "arbitrary"), vmem_limit_bytes=VMEM_LIMIT),
        name="mix_in",
    )(x, ng, win, cw, cb, wa, ba, wi, bi, lam, qg, kg, og)


def _moba_kernel(q_ref, k_ref, v_ref, km_ref, o_ref, vt_ref, ke_ref, s_ref, p_ref, cm_ref):
    pair = pl.program_id(1)
    blk = MOBA_BLOCK
    nb = km_ref.shape[0]
    seq = k_ref.shape[0]
    blk_shift = blk.bit_length() - 1

    @pl.when((pl.program_id(0) == 0) & (pair == 0))
    def _():
        krow = lax.broadcasted_iota(jnp.int32, (seq, LANES), 0)
        klane = lax.broadcasted_iota(jnp.int32, (seq, LANES), 1)
        kblk = krow >> blk_shift
        extra = jnp.where(klane == kblk, 1.0, 0.0)
        extra = jnp.where(klane == nb, (kblk << blk_shift).astype(F32), extra)
        extra = jnp.where(klane == nb + 1, _time_offset(krow).astype(F32), extra)
        ke_ref[...] = extra.astype(BF16)
        causal = (_time_offset(lax.broadcasted_iota(jnp.int32, (blk, blk), 1))
                  >= _time_offset(lax.broadcasted_iota(jnp.int32, (blk, blk), 0)))
        cm_ref[...] = jnp.where(causal, 0.0, MASK_NEG)

    vt_ref[...] = v_ref[...].astype(F32).T.astype(BF16)

    kmb = km_ref[...].astype(BF16)
    lane_head = lax.broadcasted_iota(jnp.int32, (blk, PAIR_WIDTH), 1) >> 6
    blk_row = lax.broadcasted_iota(jnp.int32, (nb, blk), 0)
    feat_row = lax.broadcasted_iota(jnp.int32, (SUBLANES, blk), 0)
    zero_rows = jnp.zeros((LANES - nb - SUBLANES, blk), F32)
    dim_head = lax.broadcasted_iota(jnp.int32, (PAIR_WIDTH, blk), 0) >> 6

    units = [(qi, hh) for qi in range(seq // blk) for hh in range(2)]
    groups = blk // SUBLANES
    qa_of, mx_of, sum_of, out_of = {}, {}, {}, {}

    def prep(u):
        qi, hh = units[u]
        head = 2 * pair + hh
        slope = lax.bitcast_convert_type(
            jnp.broadcast_to((126 - head) << 23, (SUBLANES, blk)).astype(jnp.int32), F32)
        q = q_ref[qi * blk:(qi + 1) * blk, :]
        qh = jnp.where(lane_head == hh, q, jnp.zeros_like(q))
        gate = _dot_nt(kmb, qh)
        rank = jnp.zeros((nb, blk), F32)
        for m in range(qi):
            gm = gate[m:m + 1, :]
            rank = rank + jnp.where((gm > gate) | ((gm == gate) & (blk_row > m)), 1.0, 0.0)
        keep = (blk_row == qi) | ((blk_row < qi) & (rank < float(MOBA_TOPK)))
        slope_rows = jnp.where(feat_row < 2, slope, 0.0)
        feat_t = jnp.concatenate([jnp.where(keep, 0.0, MASK_NEG), slope_rows, zero_rows], axis=0)
        qa_of[u] = jnp.concatenate([qh.astype(F32).T, feat_t], axis=0).astype(BF16)

    def score_dot(u):
        rows = (units[u][0] + 1) * blk
        ka = jnp.concatenate([k_ref[0:rows, :], ke_ref[0:rows, :]], axis=1)
        return _dot(ka, qa_of.pop(u))

    def score_chunk(u, c, s_all):
        s = s_all[c * blk:(c + 1) * blk, :]
        if c == units[u][0]:
            s = s + cm_ref[...]
        s_ref[u % 2, c * blk:(c + 1) * blk, :] = s
        part = jnp.max(s.reshape(groups, SUBLANES, blk), axis=0)
        mx_of[u] = part if c == 0 else jnp.maximum(mx_of[u], part)

    def prob_chunk(u, c, mx):
        p = jnp.exp(s_ref[u % 2, c * blk:(c + 1) * blk, :] - mx)
        p_ref[u % 2, c * blk:(c + 1) * blk, :] = p.astype(BF16)
        part = jnp.sum(p.reshape(groups, SUBLANES, blk), axis=0)
        sum_of[u] = part if c == 0 else sum_of[u] + part

    def value_stage(u):
        qi, hh = units[u]
        rows = (qi + 1) * blk
        acc = _dot(vt_ref[:, 0:rows], p_ref[u % 2, 0:rows, :])
        out_of[hh] = acc / jnp.sum(sum_of.pop(u), axis=0, keepdims=True)
        if hh == 1:
            out_t = jnp.where(dim_head == 0, out_of[0], out_of[1])
            o_ref[qi * blk:(qi + 1) * blk, :] = out_t.T

    n_units = len(units)
    for step in range(n_units + 3):
        if step < n_units:
            prep(step)
        u_score = step - 1 if 1 <= step < n_units + 1 else None
        u_prob = step - 2 if 2 <= step < n_units + 2 else None
        n_score = units[u_score][0] + 1 if u_score is not None else 0
        n_prob = units[u_prob][0] + 1 if u_prob is not None else 0
        if u_prob is not None:
            mx = jnp.max(mx_of.pop(u_prob), axis=0, keepdims=True)
        if u_score is not None:
            s_all = score_dot(u_score)
        for c in range(max(n_score, n_prob)):
            if c < n_prob:
                prob_chunk(u_prob, c, mx)
            if c < n_score:
                score_chunk(u_score, c, s_all)
        if 3 <= step:
            value_stage(step - 3)


def _moba(q, k, v, kmean):
    bsz, seq, _ = q.shape
    nb = seq // MOBA_BLOCK
    col = lambda rows: pl.BlockSpec((None, rows, PAIR_WIDTH), lambda b, p: (b, 0, p))
    return pl.pallas_call(
        _moba_kernel,
        grid=(bsz, N_PAIRS),
        in_specs=[col(seq), col(seq), col(seq), col(nb)],
        out_specs=col(seq),
        out_shape=jax.ShapeDtypeStruct((bsz, seq, ATTN_WIDTH), F32),
        scratch_shapes=[
            pltpu.VMEM((PAIR_WIDTH, seq), BF16),
            pltpu.VMEM((seq, LANES), BF16),
            pltpu.VMEM((2, seq, MOBA_BLOCK), F32),
            pltpu.VMEM((2, seq, MOBA_BLOCK), BF16),
            pltpu.VMEM((MOBA_BLOCK, MOBA_BLOCK), F32),
        ],
        compiler_params=pltpu.CompilerParams(
            dimension_semantics=("arbitrary", "arbitrary"), vmem_limit_bytes=VMEM_LIMIT),
        name="moba",
    )(q, k, v, kmean)


def _ffn_kernel(x_ref, yl_ref, ya_ref, ag_ref, wout_ref, fg_ref, wup_ref, cw_ref, cb_ref, wdn_ref,
                o_ref, up_ref, tail_ref, act_ref, stage_ref):
    t = pl.program_id(1)
    tm = TM_FFN
    n_chunks = D_FF // TF
    blocks = tm // MOBA_BLOCK
    tail_groups = FFN_CONV - 1
    tail = tail_groups * SUBLANES
    row = lax.broadcasted_iota(jnp.int32, (SUBLANES, TF), 0)

    @pl.when(t == 0)
    def _():
        tail_ref[...] = jnp.zeros(tail_ref.shape, F32)

    an = _rms(ya_ref[...], ag_ref[...]).astype(BF16)
    x1 = (_load_time_permuted(x_ref, stage_ref, blocks) + _dot(yl_ref[...], wout_ref[0:LRU_WIDTH, :])
          + _dot(an, wout_ref[LRU_WIDTH:LRU_WIDTH + ATTN_WIDTH, :]))
    hf = _rms(x1, fg_ref[...]).astype(BF16)

    def up_task(c):
        up_ref[c % 2, :, 0:TF] = _dot(hf, wup_ref[:, c * TF:(c + 1) * TF])
        up_ref[c % 2, :, TF:2 * TF] = _dot(hf, wup_ref[:, D_FF + c * TF:D_FF + (c + 1) * TF])

    def act_task(c):
        slot = c % 2

        def conv(col0, off):
            parts = []
            for b in range(blocks):
                base = b * MOBA_BLOCK
                prefix = []
                for i in range(tail_groups):
                    r_now = base + MOBA_BLOCK - tail + i * SUBLANES
                    if b:
                        before = up_ref[slot, r_now - MOBA_BLOCK:r_now - MOBA_BLOCK + SUBLANES, col0:col0 + TF]
                    else:
                        before = tail_ref[c, i * SUBLANES:(i + 1) * SUBLANES, col0:col0 + TF]
                    prefix.append(_prev_step_group(
                        before, up_ref[slot, r_now:r_now + SUBLANES, col0:col0 + TF], row))
                ext = jnp.concatenate(prefix + [up_ref[slot, base:base + MOBA_BLOCK, col0:col0 + TF]], axis=0)
                y = cb_ref[:, off:off + TF]
                for j in range(FFN_CONV):
                    y = y + cw_ref[j:j + 1, off:off + TF] * ext[j * SUBLANES:j * SUBLANES + MOBA_BLOCK, :]
                parts.append(y)
            return jnp.concatenate(parts, axis=0)

        act = jax.nn.gelu(conv(0, c * TF)) * conv(TF, D_FF + c * TF)
        act_ref[:, c * TF:(c + 1) * TF] = act.astype(BF16)
        tail_ref[c] = up_ref[slot, tm - tail:tm, :]

    up_task(0)
    for c in range(n_chunks):
        if c + 1 < n_chunks:
            up_task(c + 1)
        act_task(c)

    _store_time_unpermuted(o_ref, stage_ref, x1 + _dot(act_ref[...], wdn_ref[...]), blocks)


def _ffn(x, ylru, yattn, ag, wout, fg, wup, cw, cb, wdn):
    bsz, seq, _ = x.shape
    full = lambda shape: pl.BlockSpec(shape, lambda b, t: (0,) * len(shape), pipeline_mode=pl.Buffered(1))
    tile = lambda w: pl.BlockSpec((None, TM_FFN, w), lambda b, t: (b, t, 0))
    return pl.pallas_call(
        _ffn_kernel,
        grid=(bsz, seq // TM_FFN),
        in_specs=[
            tile(D_MODEL), tile(LRU_WIDTH), tile(ATTN_WIDTH),
            full((1, ATTN_WIDTH)),
            full((D_MODEL, D_MODEL)),
            full((1, D_MODEL)),
            full((D_MODEL, 2 * D_FF)),
            full((FFN_CONV, 2 * D_FF)),
            full((1, 2 * D_FF)),
            full((D_FF, D_MODEL)),
        ],
        out_specs=tile(D_MODEL),
        out_shape=jax.ShapeDtypeStruct((bsz, seq, D_MODEL), F32),
        scratch_shapes=[
            pltpu.VMEM((2, TM_FFN, 2 * TF), F32),
            pltpu.VMEM((D_FF // TF, (FFN_CONV - 1) * SUBLANES, 2 * TF), F32),
            pltpu.VMEM((TM_FFN, D_FF), BF16),
            pltpu.VMEM((D_MODEL // LANES, (TM_FFN // MOBA_BLOCK) * STAGE_BLOCK_ROWS, LANES), F32),
        ],
        compiler_params=pltpu.CompilerParams(
            dimension_semantics=("arbitrary", "arbitrary"), vmem_limit_bytes=VMEM_LIMIT),
        name="ffn",
    )(x, ylru, yattn, ag, wout, fg, wup, cw, cb, wdn)


def _block_diag_halves(w):
    hd = w.shape[-1]
    w4 = w.reshape(2, LRU_HEADS // 2, hd, hd)
    eye = jnp.eye(LRU_HEADS // 2, dtype=w.dtype)
    return jnp.einsum('ghij,hk->ghikj', w4, eye).reshape(2, LRU_WIDTH // 2, LRU_WIDTH // 2)


def kernel(x, mix_norm_g, w_in, lru_conv_w, lru_conv_b, lru_w_a, lru_b_a, lru_w_i, lru_b_i, lru_lambda,
           q_norm_g, k_norm_g, lru_out_g, attn_out_g, w_out, ffn_norm_g, w_up, ffn_conv_w, ffn_conv_b, w_down):
    depth = w_in.shape[0]
    bsz, seq, _ = x.shape
    assert seq % MOBA_BLOCK == 0 and seq // MOBA_BLOCK > MOBA_TOPK
    row = lambda p: p.reshape(1, -1)
    for layer in range(depth):
        ylru, q, k, v, kmean = _mix_in(
            x, row(mix_norm_g[layer]), w_in[layer].astype(BF16),
            lru_conv_w[layer], row(lru_conv_b[layer]),
            _block_diag_halves(lru_w_a[layer]).astype(BF16), row(lru_b_a[layer]),
            _block_diag_halves(lru_w_i[layer]).astype(BF16), row(lru_b_i[layer]),
            row(lru_lambda[layer]),
            row(jnp.tile(q_norm_g[layer], ATTN_HEADS)), row(jnp.tile(k_norm_g[layer], ATTN_HEADS)),
            row(lru_out_g[layer]))
        yattn = _moba(q, k, v, kmean.reshape(bsz, seq // MOBA_BLOCK, ATTN_WIDTH))
        x = _ffn(x, ylru, yattn, row(attn_out_g[layer]), w_out[layer].astype(BF16),
                 row(ffn_norm_g[layer]), w_up[layer].astype(BF16),
                 ffn_conv_w[layer], row(ffn_conv_b[layer]), w_down[layer].astype(BF16))
    return x
```

```python
import functools

import jax
import jax.numpy as jnp
from jax import lax
from jax.experimental import pallas as pl
from jax.experimental.pallas import tpu as pltpu

F32 = jnp.float32
BF16 = jnp.bfloat16

D_MODEL = 1024
LRU_WIDTH = 512
LRU_HEADS = 8
LRU_CONV = 4
LRU_C = 8.0
ATTN_WIDTH = 512
ATTN_HEADS = 8
HEAD_DIM = 64
IN_WIDTH = 2 * LRU_WIDTH + 3 * ATTN_WIDTH
MOBA_BLOCK = 256
MOBA_TOPK = 3
D_FF = 2816
FFN_CONV = 3
EPS = 1e-6

LANES = 128
SUBLANES = 8
BF16_ROWS = 16
PAIR_WIDTH = 2 * HEAD_DIM
N_PAIRS = ATTN_HEADS // 2
MASK_NEG = -1e30

TM_MIX = 4 * MOBA_BLOCK
PROJ_SLOTS = 2
TM_FFN = 512
TF = 256
VMEM_LIMIT = 56 * 1024 * 1024


def _rms(x, g):
    return x * lax.rsqrt(jnp.mean(x * x, axis=-1, keepdims=True) + EPS) * g


def _dot(a, b):
    return jnp.dot(a, b, preferred_element_type=F32)


PERM_GROUPS = MOBA_BLOCK // SUBLANES


STAGE_STRIDE = PERM_GROUPS + SUBLANES
STAGE_BLOCK_ROWS = SUBLANES * STAGE_STRIDE


def _load_time_permuted(ref, stage_ref, blocks):
    chunks = ref.shape[-1] // LANES
    for c in range(chunks):
        for b in range(blocks):
            for k in range(SUBLANES):
                src = b * MOBA_BLOCK + k * PERM_GROUPS
                dst = b * STAGE_BLOCK_ROWS + k * STAGE_STRIDE
                stage_ref[c, dst:dst + PERM_GROUPS, :] = ref[src:src + PERM_GROUPS, c * LANES:(c + 1) * LANES]
    return jnp.concatenate(
        [jnp.concatenate([stage_ref[c, pl.ds(b * STAGE_BLOCK_ROWS + g, SUBLANES, stride=STAGE_STRIDE), :]
                          for b in range(blocks) for g in range(PERM_GROUPS)], axis=0)
         for c in range(chunks)], axis=1)


def _store_time_unpermuted(ref, stage_ref, val, blocks):
    chunks = ref.shape[-1] // LANES
    for c in range(chunks):
        for b in range(blocks):
            for g in range(PERM_GROUPS):
                r0 = b * MOBA_BLOCK + g * SUBLANES
                stage_ref[c, pl.ds(b * STAGE_BLOCK_ROWS + g, SUBLANES, stride=STAGE_STRIDE), :] = (
                    val[r0:r0 + SUBLANES, c * LANES:(c + 1) * LANES])
    for c in range(chunks):
        for b in range(blocks):
            for k in range(SUBLANES):
                dst = b * MOBA_BLOCK + k * PERM_GROUPS
                src = b * STAGE_BLOCK_ROWS + k * STAGE_STRIDE
                ref[dst:dst + PERM_GROUPS, c * LANES:(c + 1) * LANES] = stage_ref[c, src:src + PERM_GROUPS, :]


def _time_offset(row):
    group = (row & (MOBA_BLOCK - 1)) >> (SUBLANES.bit_length() - 1)
    return group + (row & (SUBLANES - 1)) * PERM_GROUPS


def _prev_step_group(before, group, row):
    return jnp.where(row == 0, pltpu.roll(before, 1, 0), pltpu.roll(group, 1, 0))


def _cast_weight_once(first_step, src_ref, dst_ref):
    @pl.when(first_step)
    def _():
        def chunk(c, carry):
            cols = pl.ds(pl.multiple_of(c * LANES, LANES), LANES)
            dst_ref[:, cols] = src_ref[:, cols].astype(BF16)
            return carry
        lax.fori_loop(0, src_ref.shape[1] // LANES, chunk, 0)


def _dot_nt(a, b):
    return lax.dot_general(a, b, (((1,), (1,)), ((), ())), preferred_element_type=F32)


def _mix_in_kernel(x_ref, ng_ref, win_ref, cw_ref, cb_ref, wa_ref, ba_ref, wi_ref, bi_ref,
                   lam_ref, qg_ref, kg_ref, og_ref,
                   ylru_ref, q_ref, k_ref, v_ref, km_ref,
                   xe_ref, hc_ref, stage_ref, proj_ref, wbf_ref):
    t = pl.program_id(1)
    _cast_weight_once((pl.program_id(0) == 0) & (t == 0), win_ref, wbf_ref)
    blk = MOBA_BLOCK
    n_blocks = TM_MIX // blk
    tail = (LRU_CONV - 1) * SUBLANES
    groups = blk // SUBLANES

    @pl.when(t == 0)
    def _():
        xe_ref[...] = jnp.zeros((tail, LRU_WIDTH), F32)
        hc_ref[...] = jnp.zeros((SUBLANES, LRU_WIDTH), F32)

    x_perm = _load_time_permuted(x_ref, stage_ref, n_blocks)
    row = lax.broadcasted_iota(jnp.int32, (SUBLANES, LRU_WIDTH), 0)
    nlam = -lam_ref[...]
    softplus = jnp.maximum(nlam, 0.0) + jnp.log1p(jnp.exp(-jnp.abs(nlam)))
    ri = lax.broadcasted_iota(jnp.int32, (LANES, LANES), 0) >> 6
    ci = lax.broadcasted_iota(jnp.int32, (LANES, LANES), 1) >> 6
    head_ones = jnp.where(ri == ci, 1.0, 0.0).astype(BF16)
    base = 2 * LRU_WIDTH
    half = LRU_WIDTH // 2

    def head_norm(z, g):
        sq = (z * z).astype(BF16)
        ssq = jnp.concatenate(
            [_dot(sq[:, c * LANES:(c + 1) * LANES], head_ones) for c in range(ATTN_WIDTH // LANES)], axis=-1)
        return z * lax.rsqrt(ssq * (1.0 / HEAD_DIM) + EPS) * g

    def project_steps(b):
        rows = slice(b * blk, (b + 1) * blk)
        env = {}

        def norm():
            env['h'] = _rms(x_perm[rows, :], ng_ref[...]).astype(BF16)

        def piece(lo, hi):
            def run():
                proj_ref[b % PROJ_SLOTS, :, lo:hi] = _dot(env['h'], wbf_ref[:, lo:hi])
            return run

        def first():
            norm()
            piece(0, LRU_WIDTH)()

        def values():
            v_ref[rows, :] = _dot(env['h'], wbf_ref[:, base + 2 * ATTN_WIDTH:base + 3 * ATTN_WIDTH]).astype(BF16)

        return [first, piece(base, base + ATTN_WIDTH), piece(base + ATTN_WIDTH, base + 2 * ATTN_WIDTH),
                piece(LRU_WIDTH, base), values]

    def mix_steps(b, carried):
        rows = slice(b * blk, (b + 1) * blk)
        slot = b % PROJ_SLOTS
        env = {}

        def conv_and_gates():
            xr = proj_ref[slot, :, 0:LRU_WIDTH]
            prefix = [_prev_step_group(carried['xr'][g * SUBLANES:(g + 1) * SUBLANES, :],
                                       xr[blk - tail + g * SUBLANES:blk - tail + (g + 1) * SUBLANES, :], row)
                      for g in range(LRU_CONV - 1)]
            ext = jnp.concatenate(prefix + [xr], axis=0)
            xc = cb_ref[...]
            for j in range(LRU_CONV):
                xc = xc + cw_ref[j:j + 1, :] * ext[j * SUBLANES:j * SUBLANES + blk, :]
            carried['xr'] = xr[blk - tail:blk, :]
            xcb = xc.astype(BF16)
            env['xc'] = xc
            env['r'] = jnp.concatenate(
                [_dot(xcb[:, g * half:(g + 1) * half], wa_ref[g]) for g in range(2)], axis=-1)
            env['i'] = jnp.concatenate(
                [_dot(xcb[:, g * half:(g + 1) * half], wi_ref[g]) for g in range(2)], axis=-1)

        def decay_and_input():
            r = jax.nn.sigmoid(env.pop('r') + ba_ref[...])
            i = jax.nn.sigmoid(env.pop('i') + bi_ref[...])
            log_a = (-LRU_C * r) * softplus
            a = jnp.exp(log_a)
            one_m_a2 = 1.0 - a * a
            root = jnp.where(one_m_a2 > 0.0, one_m_a2 * lax.rsqrt(one_m_a2), 0.0)
            env['a'] = a
            env['u'] = root * (i * env.pop('xc'))

        def recurrence():
            a, u = env.pop('a'), env.pop('u')
            h_loc = [u[0:SUBLANES, :]]
            a_cum = [a[0:SUBLANES, :]]
            for g in range(1, groups):
                ag = a[g * SUBLANES:(g + 1) * SUBLANES, :]
                h_loc.append(ag * h_loc[-1] + u[g * SUBLANES:(g + 1) * SUBLANES, :])
                a_cum.append(ag * a_cum[-1])
            end_h, end_a = h_loc[-1], a_cum[-1]
            for d in (1, 2, 4):
                keep = row >= d
                a_sh = jnp.where(keep, pltpu.roll(end_a, d, 0), 1.0)
                h_sh = jnp.where(keep, pltpu.roll(end_h, d, 0), 0.0)
                end_h = end_h + end_a * h_sh
                end_a = end_a * a_sh
            carry = carried['h'][SUBLANES - 1:SUBLANES, :]
            chain_end = end_h + end_a * carry
            chain_start = jnp.where(row == 0, carry, pltpu.roll(chain_end, 1, 0))
            carried['h'] = chain_end
            env['hs'] = jnp.concatenate([h_loc[g] + a_cum[g] * chain_start for g in range(groups)], axis=0)

        def lru_out():
            yg = proj_ref[slot, :, LRU_WIDTH:base]
            ylru_ref[rows, :] = _rms(env.pop('hs') * jax.nn.gelu(yg), og_ref[...]).astype(BF16)

        def qk_out():
            q = proj_ref[slot, :, base:base + ATTN_WIDTH]
            k = proj_ref[slot, :, base + ATTN_WIDTH:base + 2 * ATTN_WIDTH]
            q_ref[rows, :] = (head_norm(q, qg_ref[...]) * (HEAD_DIM ** -0.5)).astype(BF16)
            kn = head_norm(k, kg_ref[...])
            k_ref[rows, :] = kn.astype(BF16)
            km_ref[b] = jnp.mean(kn, axis=0, keepdims=True)

        return [conv_and_gates, decay_and_input, recurrence, lru_out, qk_out]

    carried = {'xr': xe_ref[...], 'h': hc_ref[...]}
    for step in project_steps(0):
        step()
    for b in range(n_blocks):
        ahead = project_steps(b + 1) if b + 1 < n_blocks else []
        for i, step in enumerate(mix_steps(b, carried)):
            if i < len(ahead):
                ahead[i]()
            step()
    xe_ref[...] = carried['xr']
    hc_ref[...] = carried['h']


def _mix_in(x, ng, win, cw, cb, wa, ba, wi, bi, lam, qg, kg, og):
    bsz, seq, _ = x.shape
    nt = seq // TM_MIX
    full = lambda shape: pl.BlockSpec(shape, lambda b, t: (0,) * len(shape), pipeline_mode=pl.Buffered(1))
    tile = lambda w: pl.BlockSpec((None, TM_MIX, w), lambda b, t: (b, t, 0))
    return pl.pallas_call(
        _mix_in_kernel,
        grid=(bsz, nt),
        in_specs=[
            tile(D_MODEL),
            full((1, D_MODEL)),
            full((D_MODEL, IN_WIDTH)),
            full((LRU_CONV, LRU_WIDTH)),
            full((1, LRU_WIDTH)),
            full((2, LRU_WIDTH // 2, LRU_WIDTH // 2)),
            full((1, LRU_WIDTH)),
            full((2, LRU_WIDTH // 2, LRU_WIDTH // 2)),
            full((1, LRU_WIDTH)),
            full((1, LRU_WIDTH)),
            full((1, ATTN_WIDTH)),
            full((1, ATTN_WIDTH)),
            full((1, LRU_WIDTH)),
        ],
        out_specs=[
            tile(LRU_WIDTH), tile(ATTN_WIDTH), tile(ATTN_WIDTH), tile(ATTN_WIDTH),
            pl.BlockSpec((None, TM_MIX // MOBA_BLOCK, 1, ATTN_WIDTH), lambda b, t: (b, t, 0, 0)),
        ],
        out_shape=[
            jax.ShapeDtypeStruct((bsz, seq, LRU_WIDTH), BF16),
            jax.ShapeDtypeStruct((bsz, seq, ATTN_WIDTH), BF16),
            jax.ShapeDtypeStruct((bsz, seq, ATTN_WIDTH), BF16),
            jax.ShapeDtypeStruct((bsz, seq, ATTN_WIDTH), BF16),
            jax.ShapeDtypeStruct((bsz, seq // MOBA_BLOCK, 1, ATTN_WIDTH), F32),
        ],
        scratch_shapes=[
            pltpu.VMEM(((LRU_CONV - 1) * SUBLANES, LRU_WIDTH), F32),
            pltpu.VMEM((SUBLANES, LRU_WIDTH), F32),
            pltpu.VMEM((D_MODEL // LANES, (TM_MIX // MOBA_BLOCK) * STAGE_BLOCK_ROWS, LANES), F32),
            pltpu.VMEM((PROJ_SLOTS, MOBA_BLOCK, 2 * LRU_WIDTH + 2 * ATTN_WIDTH), F32),
            pltpu.VMEM((D_MODEL, IN_WIDTH), BF16),
        ],
        compiler_params=pltpu.CompilerParams(
            dimension_semantics=("arbitrary", "arbitrary"), vmem_limit_bytes=VMEM_LIMIT),
        name="mix_in",
    )(x, ng, win, cw, cb, wa, ba, wi, bi, lam, qg, kg, og)


def _moba_kernel(q_ref, k_ref, v_ref, km_ref, o_ref, vt_ref, ke_ref, s_ref, p_ref, cm_ref):
    pair = pl.program_id(1)
    blk = MOBA_BLOCK
    nb = km_ref.shape[0]
    seq = k_ref.shape[0]
    blk_shift = blk.bit_length() - 1

    @pl.when((pl.program_id(0) == 0) & (pair == 0))
    def _():
        krow = lax.broadcasted_iota(jnp.int32, (seq, LANES), 0)
        klane = lax.broadcasted_iota(jnp.int32, (seq, LANES), 1)
        kblk = krow >> blk_shift
        extra = jnp.where(klane == kblk, 1.0, 0.0)
        extra = jnp.where(klane == nb, (kblk << blk_shift).astype(F32), extra)
        extra = jnp.where(klane == nb + 1, _time_offset(krow).astype(F32), extra)
        ke_ref[...] = extra.astype(BF16)
        causal = (_time_offset(lax.broadcasted_iota(jnp.int32, (blk, blk), 1))
                  >= _time_offset(lax.broadcasted_iota(jnp.int32, (blk, blk), 0)))
        cm_ref[...] = jnp.where(causal, 0.0, MASK_NEG)

    vt_ref[...] = v_ref[...].astype(F32).T.astype(BF16)

    kmb = km_ref[...].astype(BF16)
    lane_head = lax.broadcasted_iota(jnp.int32, (blk, PAIR_WIDTH), 1) >> 6
    blk_row = lax.broadcasted_iota(jnp.int32, (nb, blk), 0)
    feat_row = lax.broadcasted_iota(jnp.int32, (SUBLANES, blk), 0)
    zero_rows = jnp.zeros((LANES - nb - SUBLANES, blk), F32)
    dim_head = lax.broadcasted_iota(jnp.int32, (PAIR_WIDTH, blk), 0) >> 6

    units = [(qi, hh) for qi in range(seq // blk) for hh in range(2)]
    groups = blk // SUBLANES
    qa_of, mx_of, sum_of, out_of = {}, {}, {}, {}

    def prep(u):
        qi, hh = units[u]
        head = 2 * pair + hh
        slope = lax.bitcast_convert_type(
            jnp.broadcast_to((126 - head) << 23, (SUBLANES, blk)).astype(jnp.int32), F32)
        q = q_ref[qi * blk:(qi + 1) * blk, :]
        qh = jnp.where(lane_head == hh, q, jnp.zeros_like(q))
        gate = _dot_nt(kmb, qh)
        rank = jnp.zeros((nb, blk), F32)
        for m in range(qi):
            gm = gate[m:m + 1, :]
            rank = rank + jnp.where((gm > gate) | ((gm == gate) & (blk_row > m)), 1.0, 0.0)
        keep = (blk_row == qi) | ((blk_row < qi) & (rank < float(MOBA_TOPK)))
        slope_rows = jnp.where(feat_row < 2, slope, 0.0)
        feat_t = jnp.concatenate([jnp.where(keep, 0.0, MASK_NEG), slope_rows, zero_rows], axis=0)
        qa_of[u] = jnp.concatenate([qh.astype(F32).T, feat_t], axis=0).astype(BF16)

    def score_dot(u):
        n_chunks = units[u][0] + 1
        qa = qa_of.pop(u)
        halves = [(0, (n_chunks + 1) // 2), ((n_chunks + 1) // 2, n_chunks)]
        parts = []
        for lo, hi in halves:
            if hi > lo:
                ka = jnp.concatenate([k_ref[lo * blk:hi * blk, :], ke_ref[lo * blk:hi * blk, :]], axis=1)
                parts.append(_dot(ka, qa))
        return jnp.concatenate(parts, axis=0)

    def score_chunk(u, c, s_all):
        s = s_all[c * blk:(c + 1) * blk, :]
        if c == units[u][0]:
            s = s + cm_ref[...]
        s_ref[u % 2, c * blk:(c + 1) * blk, :] = s
        part = jnp.max(s.reshape(groups, SUBLANES, blk), axis=0)
        mx_of[u] = part if c == 0 else jnp.maximum(mx_of[u], part)

    def prob_chunk(u, c, mx):
        p = jnp.exp(s_ref[u % 2, c * blk:(c + 1) * blk, :] - mx)
        p_ref[u % 2, c * blk:(c + 1) * blk, :] = p.astype(BF16)
        part = jnp.sum(p.reshape(groups, SUBLANES, blk), axis=0)
        sum_of[u] = part if c == 0 else sum_of[u] + part

    def value_stage(u):
        qi, hh = units[u]
        rows = (qi + 1) * blk
        acc = _dot(vt_ref[:, 0:rows], p_ref[u % 2, 0:rows, :])
        out_of[hh] = acc / jnp.sum(sum_of.pop(u), axis=0, keepdims=True)
        if hh == 1:
            out_t = jnp.where(dim_head == 0, out_of[0], out_of[1])
            o_ref[qi * blk:(qi + 1) * blk, :] = out_t.T

    n_units = len(units)
    for step in range(n_units + 3):
        if step < n_units:
            prep(step)
        u_score = step - 1 if 1 <= step < n_units + 1 else None
        u_prob = step - 2 if 2 <= step < n_units + 2 else None
        n_score = units[u_score][0] + 1 if u_score is not None else 0
        n_prob = units[u_prob][0] + 1 if u_prob is not None else 0
        if u_prob is not None:
            mx = jnp.max(mx_of.pop(u_prob), axis=0, keepdims=True)
        if u_score is not None:
            s_all = score_dot(u_score)
        for c in range(max(n_score, n_prob)):
            if c < n_prob:
                prob_chunk(u_prob, c, mx)
            if c < n_score:
                score_chunk(u_score, c, s_all)
        if 3 <= step:
            value_stage(step - 3)


def _moba(q, k, v, kmean):
    bsz, seq, _ = q.shape
    nb = seq // MOBA_BLOCK
    col = lambda rows: pl.BlockSpec((None, rows, PAIR_WIDTH), lambda b, p: (b, 0, p))
    return pl.pallas_call(
        _moba_kernel,
        grid=(bsz, N_PAIRS),
        in_specs=[col(seq), col(seq), col(seq), col(nb)],
        out_specs=col(seq),
        out_shape=jax.ShapeDtypeStruct((bsz, seq, ATTN_WIDTH), F32),
        scratch_shapes=[
            pltpu.VMEM((PAIR_WIDTH, seq), BF16),
            pltpu.VMEM((seq, LANES), BF16),
            pltpu.VMEM((2, seq, MOBA_BLOCK), F32),
            pltpu.VMEM((2, seq, MOBA_BLOCK), BF16),
            pltpu.VMEM((MOBA_BLOCK, MOBA_BLOCK), F32),
        ],
        compiler_params=pltpu.CompilerParams(
            dimension_semantics=("arbitrary", "arbitrary"), vmem_limit_bytes=VMEM_LIMIT),
        name="moba",
    )(q, k, v, kmean)


def _ffn_kernel(x_ref, yl_ref, ya_ref, ag_ref, wout_ref, fg_ref, wup_ref, cw_ref, cb_ref, wdn_ref,
                o_ref, up_ref, tail_ref, act_ref, stage_ref, wout_bf_ref, wdn_bf_ref):
    t = pl.program_id(1)
    first_step = (pl.program_id(0) == 0) & (t == 0)
    _cast_weight_once(first_step, wout_ref, wout_bf_ref)
    _cast_weight_once(first_step, wdn_ref, wdn_bf_ref)
    tm = TM_FFN
    n_chunks = D_FF // TF
    blocks = tm // MOBA_BLOCK
    tail_groups = FFN_CONV - 1
    tail = tail_groups * SUBLANES
    row = lax.broadcasted_iota(jnp.int32, (SUBLANES, TF), 0)

    @pl.when(t == 0)
    def _():
        tail_ref[...] = jnp.zeros(tail_ref.shape, F32)

    an = _rms(ya_ref[...], ag_ref[...]).astype(BF16)
    x1 = (_load_time_permuted(x_ref, stage_ref, blocks) + _dot(yl_ref[...], wout_bf_ref[0:LRU_WIDTH, :])
          + _dot(an, wout_bf_ref[LRU_WIDTH:LRU_WIDTH + ATTN_WIDTH, :]))
    hf = _rms(x1, fg_ref[...]).astype(BF16)

    def up_task(c):
        up_ref[c % 2, :, 0:TF] = _dot(hf, wup_ref[:, c * TF:(c + 1) * TF])
        up_ref[c % 2, :, TF:2 * TF] = _dot(hf, wup_ref[:, D_FF + c * TF:D_FF + (c + 1) * TF])

    def act_task(c):
        slot = c % 2

        def conv(col0, off):
            parts = []
            for b in range(blocks):
                base = b * MOBA_BLOCK
                prefix = []
                for i in range(tail_groups):
                    r_now = base + MOBA_BLOCK - tail + i * SUBLANES
                    if b:
                        before = up_ref[slot, r_now - MOBA_BLOCK:r_now - MOBA_BLOCK + SUBLANES, col0:col0 + TF]
                    else:
                        before = tail_ref[c, i * SUBLANES:(i + 1) * SUBLANES, col0:col0 + TF]
                    prefix.append(_prev_step_group(
                        before, up_ref[slot, r_now:r_now + SUBLANES, col0:col0 + TF], row))
                ext = jnp.concatenate(prefix + [up_ref[slot, base:base + MOBA_BLOCK, col0:col0 + TF]], axis=0)
                y = cb_ref[:, off:off + TF]
                for j in range(FFN_CONV):
                    y = y + cw_ref[j:j + 1, off:off + TF] * ext[j * SUBLANES:j * SUBLANES + MOBA_BLOCK, :]
                parts.append(y)
            return jnp.concatenate(parts, axis=0)

        act = jax.nn.gelu(conv(0, c * TF)) * conv(TF, D_FF + c * TF)
        act_ref[:, c * TF:(c + 1) * TF] = act.astype(BF16)
        tail_ref[c] = up_ref[slot, tm - tail:tm, :]

    up_task(0)
    for c in range(n_chunks):
        if c + 1 < n_chunks:
            up_task(c + 1)
        act_task(c)

    _store_time_unpermuted(o_ref, stage_ref, x1 + _dot(act_ref[...], wdn_bf_ref[...]), blocks)


def _ffn(x, ylru, yattn, ag, wout, fg, wup, cw, cb, wdn):
    bsz, seq, _ = x.shape
    full = lambda shape: pl.BlockSpec(shape, lambda b, t: (0,) * len(shape), pipeline_mode=pl.Buffered(1))
    tile = lambda w: pl.BlockSpec((None, TM_FFN, w), lambda b, t: (b, t, 0))
    return pl.pallas_call(
        _ffn_kernel,
        grid=(bsz, seq // TM_FFN),
        in_specs=[
            tile(D_MODEL), tile(LRU_WIDTH), tile(ATTN_WIDTH),
            full((1, ATTN_WIDTH)),
            full((D_MODEL, D_MODEL)),
            full((1, D_MODEL)),
            full((D_MODEL, 2 * D_FF)),
            full((FFN_CONV, 2 * D_FF)),
            full((1, 2 * D_FF)),
            full((D_FF, D_MODEL)),
        ],
        out_specs=tile(D_MODEL),
        out_shape=jax.ShapeDtypeStruct((bsz, seq, D_MODEL), F32),
        scratch_shapes=[
            pltpu.VMEM((2, TM_FFN, 2 * TF), F32),
            pltpu.VMEM((D_FF // TF, (FFN_CONV - 1) * SUBLANES, 2 * TF), F32),
            pltpu.VMEM((TM_FFN, D_FF), BF16),
            pltpu.VMEM((D_MODEL // LANES, (TM_FFN // MOBA_BLOCK) * STAGE_BLOCK_ROWS, LANES), F32),
            pltpu.VMEM((D_MODEL, D_MODEL), BF16),
            pltpu.VMEM((D_FF, D_MODEL), BF16),
        ],
        compiler_params=pltpu.CompilerParams(
            dimension_semantics=("arbitrary", "arbitrary"), vmem_limit_bytes=VMEM_LIMIT),
        name="ffn",
    )(x, ylru, yattn, ag, wout, fg, wup, cw, cb, wdn)


def _block_diag_halves(w):
    hd = w.shape[-1]
    w4 = w.reshape(2, LRU_HEADS // 2, hd, hd)
    eye = jnp.eye(LRU_HEADS // 2, dtype=w.dtype)
    return jnp.einsum('ghij,hk->ghikj', w4, eye).reshape(2, LRU_WIDTH // 2, LRU_WIDTH // 2)


def kernel(x, mix_norm_g, w_in, lru_conv_w, lru_conv_b, lru_w_a, lru_b_a, lru_w_i, lru_b_i, lru_lambda,
           q_norm_g, k_norm_g, lru_out_g, attn_out_g, w_out, ffn_norm_g, w_up, ffn_conv_w, ffn_conv_b, w_down):
    depth = w_in.shape[0]
    bsz, seq, _ = x.shape
    assert seq % MOBA_BLOCK == 0 and seq // MOBA_BLOCK > MOBA_TOPK
    row = lambda p: p.reshape(1, -1)
    for layer in range(depth):
        ylru, q, k, v, kmean = _mix_in(
            x, row(mix_norm_g[layer]), w_in[layer],
            lru_conv_w[layer], row(lru_conv_b[layer]),
            _block_diag_halves(lru_w_a[layer]).astype(BF16), row(lru_b_a[layer]),
            _block_diag_halves(lru_w_i[layer]).astype(BF16), row(lru_b_i[layer]),
            row(lru_lambda[layer]),
            row(jnp.tile(q_norm_g[layer], ATTN_HEADS)), row(jnp.tile(k_norm_g[layer], ATTN_HEADS)),
            row(lru_out_g[layer]))
        yattn = _moba(q, k, v, kmean.reshape(bsz, seq // MOBA_BLOCK, ATTN_WIDTH))
        x = _ffn(x, ylru, yattn, row(attn_out_g[layer]), w_out[layer],
                 row(ffn_norm_g[layer]), w_up[layer].astype(BF16),
                 ffn_conv_w[layer], row(ffn_conv_b[layer]), w_down[layer])
    return x
```

```python
import jax
import jax.numpy as jnp
from jax import lax
from jax.experimental import pallas as pl
from jax.experimental.pallas import tpu as pltpu

F32 = jnp.float32
BF16 = jnp.bfloat16

D_MODEL = 1024
LRU_WIDTH = 512
LRU_HEADS = 8
LRU_CONV = 4
LRU_C = 8.0
ATTN_WIDTH = 512
ATTN_HEADS = 8
HEAD_DIM = 64
IN_WIDTH = 2 * LRU_WIDTH + 3 * ATTN_WIDTH
MOBA_BLOCK = 256
MOBA_TOPK = 3
D_FF = 2816
FFN_CONV = 3
EPS = 1e-6

LANES = 128
SUBLANES = 8
PAIR_WIDTH = 2 * HEAD_DIM
HEAD_SHIFT = HEAD_DIM.bit_length() - 1
N_PAIRS = ATTN_HEADS // 2
MASK_NEG = -1e30

TM_MIX = 4 * MOBA_BLOCK
SCORE_DOTS = 2
PROJ_SLOTS = 2
TM_FFN = 512
TF = 256
W_CHUNK = 128
VMEM_LIMIT = 56 * 1024 * 1024


def _rms(x, g):
    return x * lax.rsqrt(jnp.mean(x * x, axis=-1, keepdims=True) + EPS) * g


def _dot(a, b):
    return jnp.dot(a, b, preferred_element_type=F32)


PERM_GROUPS = MOBA_BLOCK // SUBLANES


STAGE_STRIDE = PERM_GROUPS + SUBLANES
STAGE_BLOCK_ROWS = SUBLANES * STAGE_STRIDE


def _load_time_permuted(ref, stage_ref, blocks):
    chunks = ref.shape[-1] // LANES
    for c in range(chunks):
        for b in range(blocks):
            for k in range(SUBLANES):
                src = b * MOBA_BLOCK + k * PERM_GROUPS
                dst = b * STAGE_BLOCK_ROWS + k * STAGE_STRIDE
                stage_ref[c, dst:dst + PERM_GROUPS, :] = ref[src:src + PERM_GROUPS, c * LANES:(c + 1) * LANES]
    return jnp.concatenate(
        [jnp.concatenate([stage_ref[c, pl.ds(b * STAGE_BLOCK_ROWS + g, SUBLANES, stride=STAGE_STRIDE), :]
                          for b in range(blocks) for g in range(PERM_GROUPS)], axis=0)
         for c in range(chunks)], axis=1)


def _store_time_unpermuted(ref, stage_ref, val, blocks):
    chunks = ref.shape[-1] // LANES
    for c in range(chunks):
        for b in range(blocks):
            for g in range(PERM_GROUPS):
                r0 = b * MOBA_BLOCK + g * SUBLANES
                stage_ref[c, pl.ds(b * STAGE_BLOCK_ROWS + g, SUBLANES, stride=STAGE_STRIDE), :] = (
                    val[r0:r0 + SUBLANES, c * LANES:(c + 1) * LANES])
    for c in range(chunks):
        for b in range(blocks):
            for k in range(SUBLANES):
                dst = b * MOBA_BLOCK + k * PERM_GROUPS
                src = b * STAGE_BLOCK_ROWS + k * STAGE_STRIDE
                ref[dst:dst + PERM_GROUPS, c * LANES:(c + 1) * LANES] = stage_ref[c, src:src + PERM_GROUPS, :]


def _time_offset(row):
    group = (row & (MOBA_BLOCK - 1)) >> (SUBLANES.bit_length() - 1)
    return group + (row & (SUBLANES - 1)) * PERM_GROUPS


def _prev_step_group(before, group, row):
    return jnp.where(row == 0, pltpu.roll(before, 1, 0), pltpu.roll(group, 1, 0))


def _cast_weight_once(first_step, src_ref, dst_ref):
    @pl.when(first_step)
    def _():
        def chunk(c, carry):
            cols = pl.ds(pl.multiple_of(c * LANES, LANES), LANES)
            dst_ref[:, cols] = src_ref[:, cols].astype(BF16)
            return carry
        lax.fori_loop(0, src_ref.shape[1] // LANES, chunk, 0)


def _stream_cast_weight(first_step, src_hbm, dst_ref, stage_ref, sem):
    n_chunks = src_hbm.shape[1] // W_CHUNK

    def copy(i, slot):
        cols = pl.ds(pl.multiple_of(i * W_CHUNK, W_CHUNK), W_CHUNK)
        return pltpu.make_async_copy(src_hbm.at[:, cols], stage_ref.at[slot], sem.at[slot])

    @pl.when(first_step)
    def _():
        copy(0, 0).start()

        def body(i, carry):
            slot = i % 2

            @pl.when(i + 1 < n_chunks)
            def _():
                copy(i + 1, 1 - slot).start()

            copy(i, slot).wait()
            cols = pl.ds(pl.multiple_of(i * W_CHUNK, W_CHUNK), W_CHUNK)
            dst_ref[:, cols] = stage_ref[slot].astype(BF16)
            return carry

        lax.fori_loop(0, n_chunks, body, 0)


def _dot_nt(a, b):
    return lax.dot_general(a, b, (((1,), (1,)), ((), ())), preferred_element_type=F32)


def _mix_in_kernel(x_ref, ng_ref, win_ref, cw_ref, cb_ref, wa_ref, ba_ref, wi_ref, bi_ref,
                   lam_ref, qg_ref, kg_ref, og_ref,
                   ylru_ref, q_ref, k_ref, v_ref, km_ref,
                   xe_ref, hc_ref, stage_ref, proj_ref, wbf_ref):
    t = pl.program_id(1)
    _cast_weight_once((pl.program_id(0) == 0) & (t == 0), win_ref, wbf_ref)
    blk = MOBA_BLOCK
    n_blocks = TM_MIX // blk
    tail = (LRU_CONV - 1) * SUBLANES
    groups = blk // SUBLANES

    @pl.when(t == 0)
    def _():
        xe_ref[...] = jnp.zeros((tail, LRU_WIDTH), F32)
        hc_ref[...] = jnp.zeros((SUBLANES, LRU_WIDTH), F32)

    x_perm = _load_time_permuted(x_ref, stage_ref, n_blocks)
    row = lax.broadcasted_iota(jnp.int32, (SUBLANES, LRU_WIDTH), 0)
    nlam = -lam_ref[...]
    softplus = jnp.maximum(nlam, 0.0) + jnp.log1p(jnp.exp(-jnp.abs(nlam)))
    ri = lax.broadcasted_iota(jnp.int32, (LANES, LANES), 0) >> HEAD_SHIFT
    ci = lax.broadcasted_iota(jnp.int32, (LANES, LANES), 1) >> HEAD_SHIFT
    head_ones = jnp.where(ri == ci, 1.0, 0.0).astype(BF16)
    base = 2 * LRU_WIDTH
    half = LRU_WIDTH // 2

    def head_norm(z, g):
        sq = (z * z).astype(BF16)
        ssq = jnp.concatenate(
            [_dot(sq[:, c * LANES:(c + 1) * LANES], head_ones) for c in range(ATTN_WIDTH // LANES)], axis=-1)
        return z * lax.rsqrt(ssq * (1.0 / HEAD_DIM) + EPS) * g

    def project_steps(b):
        rows = slice(b * blk, (b + 1) * blk)
        env = {}

        def norm():
            env['h'] = _rms(x_perm[rows, :], ng_ref[...]).astype(BF16)

        def piece(lo, hi):
            def run():
                proj_ref[b % PROJ_SLOTS, :, lo:hi] = _dot(env['h'], wbf_ref[:, lo:hi])
            return run

        def first():
            norm()
            piece(0, LRU_WIDTH)()

        def values():
            v_ref[rows, :] = _dot(env['h'], wbf_ref[:, base + 2 * ATTN_WIDTH:base + 3 * ATTN_WIDTH]).astype(BF16)

        return [first, piece(base, base + ATTN_WIDTH), piece(base + ATTN_WIDTH, base + 2 * ATTN_WIDTH),
                piece(LRU_WIDTH, base), values]

    def mix_steps(b, carried):
        rows = slice(b * blk, (b + 1) * blk)
        slot = b % PROJ_SLOTS
        env = {}

        def conv_and_gates():
            xr = proj_ref[slot, :, 0:LRU_WIDTH]
            prefix = [_prev_step_group(carried['xr'][g * SUBLANES:(g + 1) * SUBLANES, :],
                                       xr[blk - tail + g * SUBLANES:blk - tail + (g + 1) * SUBLANES, :], row)
                      for g in range(LRU_CONV - 1)]
            ext = jnp.concatenate(prefix + [xr], axis=0)
            xc = cb_ref[...]
            for j in range(LRU_CONV):
                xc = xc + cw_ref[j:j + 1, :] * ext[j * SUBLANES:j * SUBLANES + blk, :]
            carried['xr'] = xr[blk - tail:blk, :]
            xcb = xc.astype(BF16)
            env['xc'] = xc
            env['r'] = jnp.concatenate(
                [_dot(xcb[:, g * half:(g + 1) * half], wa_ref[g]) for g in range(2)], axis=-1)
            env['i'] = jnp.concatenate(
                [_dot(xcb[:, g * half:(g + 1) * half], wi_ref[g]) for g in range(2)], axis=-1)

        def decay_and_input():
            r = jax.nn.sigmoid(env.pop('r') + ba_ref[...])
            i = jax.nn.sigmoid(env.pop('i') + bi_ref[...])
            log_a = (-LRU_C * r) * softplus
            a = jnp.exp(log_a)
            one_m_a2 = 1.0 - a * a
            root = jnp.where(one_m_a2 > 0.0, one_m_a2 * lax.rsqrt(one_m_a2), 0.0)
            env['a'] = a
            env['u'] = root * (i * env.pop('xc'))

        def recurrence():
            a, u = env.pop('a'), env.pop('u')
            h_loc = [u[0:SUBLANES, :]]
            a_cum = [a[0:SUBLANES, :]]
            for g in range(1, groups):
                ag = a[g * SUBLANES:(g + 1) * SUBLANES, :]
                h_loc.append(ag * h_loc[-1] + u[g * SUBLANES:(g + 1) * SUBLANES, :])
                a_cum.append(ag * a_cum[-1])
            end_h, end_a = h_loc[-1], a_cum[-1]
            for d in (1, 2, 4):
                keep = row >= d
                a_sh = jnp.where(keep, pltpu.roll(end_a, d, 0), 1.0)
                h_sh = jnp.where(keep, pltpu.roll(end_h, d, 0), 0.0)
                end_h = end_h + end_a * h_sh
                end_a = end_a * a_sh
            carry = carried['h'][SUBLANES - 1:SUBLANES, :]
            chain_end = end_h + end_a * carry
            chain_start = jnp.where(row == 0, carry, pltpu.roll(chain_end, 1, 0))
            carried['h'] = chain_end
            env['hs'] = jnp.concatenate([h_loc[g] + a_cum[g] * chain_start for g in range(groups)], axis=0)

        def lru_out():
            yg = proj_ref[slot, :, LRU_WIDTH:base]
            ylru_ref[rows, :] = _rms(env.pop('hs') * jax.nn.gelu(yg), og_ref[...]).astype(BF16)

        def qk_out():
            q = proj_ref[slot, :, base:base + ATTN_WIDTH]
            k = proj_ref[slot, :, base + ATTN_WIDTH:base + 2 * ATTN_WIDTH]
            q_ref[rows, :] = (head_norm(q, qg_ref[...]) * (HEAD_DIM ** -0.5)).astype(BF16)
            kn = head_norm(k, kg_ref[...])
            k_ref[rows, :] = kn.astype(BF16)
            km_ref[b] = jnp.mean(kn, axis=0, keepdims=True)

        return [conv_and_gates, decay_and_input, recurrence, lru_out, qk_out]

    carried = {'xr': xe_ref[...], 'h': hc_ref[...]}
    for step in project_steps(0):
        step()
    for b in range(n_blocks):
        ahead = project_steps(b + 1) if b + 1 < n_blocks else []
        for i, step in enumerate(mix_steps(b, carried)):
            if i < len(ahead):
                ahead[i]()
            step()
    xe_ref[...] = carried['xr']
    hc_ref[...] = carried['h']


def _mix_in(x, ng, win, cw, cb, wa, ba, wi, bi, lam, qg, kg, og):
    bsz, seq, _ = x.shape
    nt = seq // TM_MIX
    full = lambda shape: pl.BlockSpec(shape, lambda b, t: (0,) * len(shape), pipeline_mode=pl.Buffered(1))
    tile = lambda w: pl.BlockSpec((None, TM_MIX, w), lambda b, t: (b, t, 0))
    return pl.pallas_call(
        _mix_in_kernel,
        grid=(bsz, nt),
        in_specs=[
            tile(D_MODEL),
            full((1, D_MODEL)),
            full((D_MODEL, IN_WIDTH)),
            full((LRU_CONV, LRU_WIDTH)),
            full((1, LRU_WIDTH)),
            full((2, LRU_WIDTH // 2, LRU_WIDTH // 2)),
            full((1, LRU_WIDTH)),
            full((2, LRU_WIDTH // 2, LRU_WIDTH // 2)),
            full((1, LRU_WIDTH)),
            full((1, LRU_WIDTH)),
            full((1, ATTN_WIDTH)),
            full((1, ATTN_WIDTH)),
            full((1, LRU_WIDTH)),
        ],
        out_specs=[
            tile(LRU_WIDTH), tile(ATTN_WIDTH), tile(ATTN_WIDTH), tile(ATTN_WIDTH),
            pl.BlockSpec((None, TM_MIX // MOBA_BLOCK, 1, ATTN_WIDTH), lambda b, t: (b, t, 0, 0)),
        ],
        out_shape=[
            jax.ShapeDtypeStruct((bsz, seq, LRU_WIDTH), BF16),
            jax.ShapeDtypeStruct((bsz, seq, ATTN_WIDTH), BF16),
            jax.ShapeDtypeStruct((bsz, seq, ATTN_WIDTH), BF16),
            jax.ShapeDtypeStruct((bsz, seq, ATTN_WIDTH), BF16),
            jax.ShapeDtypeStruct((bsz, seq // MOBA_BLOCK, 1, ATTN_WIDTH), F32),
        ],
        scratch_shapes=[
            pltpu.VMEM(((LRU_CONV - 1) * SUBLANES, LRU_WIDTH), F32),
            pltpu.VMEM((SUBLANES, LRU_WIDTH), F32),
            pltpu.VMEM((D_MODEL // LANES, (TM_MIX // MOBA_BLOCK) * STAGE_BLOCK_ROWS, LANES), F32),
            pltpu.VMEM((PROJ_SLOTS, MOBA_BLOCK, 2 * LRU_WIDTH + 2 * ATTN_WIDTH), F32),
            pltpu.VMEM((D_MODEL, IN_WIDTH), BF16),
        ],
        compiler_params=pltpu.CompilerParams(
            dimension_semantics=("arbitrary", "arbitrary"), vmem_limit_bytes=VMEM_LIMIT),
        name="mix_in",
    )(x, ng, win, cw, cb, wa, ba, wi, bi, lam, qg, kg, og)


def _moba_kernel(q_ref, k_ref, v_ref, km_ref, o_ref, vt_ref, ke_ref, s_ref, p_ref, cm_ref):
    pair = pl.program_id(1)
    blk = MOBA_BLOCK
    nb = km_ref.shape[0]
    seq = k_ref.shape[0]
    blk_shift = blk.bit_length() - 1

    @pl.when((pl.program_id(0) == 0) & (pair == 0))
    def _():
        krow = lax.broadcasted_iota(jnp.int32, (seq, LANES), 0)
        klane = lax.broadcasted_iota(jnp.int32, (seq, LANES), 1)
        kblk = krow >> blk_shift
        extra = jnp.where(klane == kblk, 1.0, 0.0)
        extra = jnp.where(klane == nb, (kblk << blk_shift).astype(F32), extra)
        extra = jnp.where(klane == nb + 1, _time_offset(krow).astype(F32), extra)
        ke_ref[...] = extra.astype(BF16)
        causal = (_time_offset(lax.broadcasted_iota(jnp.int32, (blk, blk), 1))
                  >= _time_offset(lax.broadcasted_iota(jnp.int32, (blk, blk), 0)))
        cm_ref[...] = jnp.where(causal, 0.0, MASK_NEG)

    vt_ref[...] = v_ref[...].astype(F32).T.astype(BF16)

    kmb = km_ref[...].astype(BF16)
    lane_head = lax.broadcasted_iota(jnp.int32, (blk, PAIR_WIDTH), 1) >> HEAD_SHIFT
    blk_row = lax.broadcasted_iota(jnp.int32, (nb, blk), 0)
    feat_row = lax.broadcasted_iota(jnp.int32, (SUBLANES, blk), 0)
    zero_rows = jnp.zeros((LANES - nb - SUBLANES, blk), F32)
    dim_head = lax.broadcasted_iota(jnp.int32, (PAIR_WIDTH, blk), 0) >> HEAD_SHIFT

    units = [(qi, hh) for qi in range(seq // blk) for hh in range(2)]
    groups = blk // SUBLANES
    qa_of, mx_of, sum_of, out_of = {}, {}, {}, {}

    def prep(u):
        qi, hh = units[u]
        head = 2 * pair + hh
        slope = lax.bitcast_convert_type(
            jnp.broadcast_to((126 - head) << 23, (SUBLANES, blk)).astype(jnp.int32), F32)
        q = q_ref[qi * blk:(qi + 1) * blk, :]
        qh = jnp.where(lane_head == hh, q, jnp.zeros_like(q))
        gate = _dot_nt(kmb, qh)
        rank = jnp.zeros((nb, blk), F32)
        for m in range(qi):
            gm = gate[m:m + 1, :]
            rank = rank + jnp.where((gm > gate) | ((gm == gate) & (blk_row > m)), 1.0, 0.0)
        keep = (blk_row == qi) | ((blk_row < qi) & (rank < float(MOBA_TOPK)))
        slope_rows = jnp.where(feat_row < 2, slope, 0.0)
        feat_t = jnp.concatenate([jnp.where(keep, 0.0, MASK_NEG), slope_rows, zero_rows], axis=0)
        qa_of[u] = jnp.concatenate([qh.astype(F32).T, feat_t], axis=0).astype(BF16)

    def score_dot(u):
        n_chunks = units[u][0] + 1
        qa = qa_of.pop(u)
        n_dots = min(SCORE_DOTS, n_chunks)
        bounds = [-(-n_chunks * i // n_dots) for i in range(n_dots + 1)]
        parts = []
        for lo, hi in zip(bounds[:-1], bounds[1:]):
            if hi > lo:
                ka = jnp.concatenate([k_ref[lo * blk:hi * blk, :], ke_ref[lo * blk:hi * blk, :]], axis=1)
                parts.append(_dot(ka, qa))
        return jnp.concatenate(parts, axis=0)

    def score_chunk(u, c, s_all):
        s = s_all[c * blk:(c + 1) * blk, :]
        if c == units[u][0]:
            s = s + cm_ref[...]
        s_ref[u % 2, c * blk:(c + 1) * blk, :] = s
        part = jnp.max(s.reshape(groups, SUBLANES, blk), axis=0)
        mx_of[u] = part if c == 0 else jnp.maximum(mx_of[u], part)

    def prob_chunk(u, c, mx):
        p = jnp.exp(s_ref[u % 2, c * blk:(c + 1) * blk, :] - mx)
        p_ref[u % 2, c * blk:(c + 1) * blk, :] = p.astype(BF16)
        part = jnp.sum(p.reshape(groups, SUBLANES, blk), axis=0)
        sum_of[u] = part if c == 0 else sum_of[u] + part

    def value_stage(u):
        qi, hh = units[u]
        rows = (qi + 1) * blk
        acc = _dot(vt_ref[:, 0:rows], p_ref[u % 2, 0:rows, :])
        out_of[hh] = acc / jnp.sum(sum_of.pop(u), axis=0, keepdims=True)
        if hh == 1:
            out_t = jnp.where(dim_head == 0, out_of[0], out_of[1])
            o_ref[qi * blk:(qi + 1) * blk, :] = out_t.T

    n_units = len(units)
    for step in range(n_units + 3):
        if step < n_units:
            prep(step)
        u_score = step - 1 if 1 <= step < n_units + 1 else None
        u_prob = step - 2 if 2 <= step < n_units + 2 else None
        n_score = units[u_score][0] + 1 if u_score is not None else 0
        n_prob = units[u_prob][0] + 1 if u_prob is not None else 0
        if u_prob is not None:
            mx = jnp.max(mx_of.pop(u_prob), axis=0, keepdims=True)
        if u_score is not None:
            s_all = score_dot(u_score)
        for c in range(max(n_score, n_prob)):
            if c < n_prob:
                prob_chunk(u_prob, c, mx)
            if c < n_score:
                score_chunk(u_score, c, s_all)
        if 3 <= step:
            value_stage(step - 3)


def _moba(q, k, v, kmean):
    bsz, seq, _ = q.shape
    nb = seq // MOBA_BLOCK
    col = lambda rows: pl.BlockSpec((None, rows, PAIR_WIDTH), lambda b, p: (b, 0, p))
    return pl.pallas_call(
        _moba_kernel,
        grid=(bsz, N_PAIRS),
        in_specs=[col(seq), col(seq), col(seq), col(nb)],
        out_specs=col(seq),
        out_shape=jax.ShapeDtypeStruct((bsz, seq, ATTN_WIDTH), F32),
        scratch_shapes=[
            pltpu.VMEM((PAIR_WIDTH, seq), BF16),
            pltpu.VMEM((seq, LANES), BF16),
            pltpu.VMEM((2, seq, MOBA_BLOCK), F32),
            pltpu.VMEM((2, seq, MOBA_BLOCK), BF16),
            pltpu.VMEM((MOBA_BLOCK, MOBA_BLOCK), F32),
        ],
        compiler_params=pltpu.CompilerParams(
            dimension_semantics=("arbitrary", "arbitrary"), vmem_limit_bytes=VMEM_LIMIT),
        name="moba",
    )(q, k, v, kmean)


def _ffn_kernel(x_ref, yl_ref, ya_ref, ag_ref, wout_ref, fg_ref, wup_ref, cw_ref, cb_ref, wdn_ref,
                o_ref, up_ref, tail_ref, act_ref, stage_ref, wout_bf_ref, wdn_bf_ref,
                wup_bf_ref, wstage_ref, wsem):
    t = pl.program_id(1)
    first_step = (pl.program_id(0) == 0) & (t == 0)
    _cast_weight_once(first_step, wout_ref, wout_bf_ref)
    _cast_weight_once(first_step, wdn_ref, wdn_bf_ref)
    _stream_cast_weight(first_step, wup_ref, wup_bf_ref, wstage_ref, wsem)
    tm = TM_FFN
    n_chunks = D_FF // TF
    blocks = tm // MOBA_BLOCK
    tail_groups = FFN_CONV - 1
    tail = tail_groups * SUBLANES
    row = lax.broadcasted_iota(jnp.int32, (SUBLANES, TF), 0)

    @pl.when(t == 0)
    def _():
        tail_ref[...] = jnp.zeros(tail_ref.shape, F32)

    an = _rms(ya_ref[...], ag_ref[...]).astype(BF16)
    x1 = (_load_time_permuted(x_ref, stage_ref, blocks) + _dot(yl_ref[...], wout_bf_ref[0:LRU_WIDTH, :])
          + _dot(an, wout_bf_ref[LRU_WIDTH:LRU_WIDTH + ATTN_WIDTH, :]))
    hf = _rms(x1, fg_ref[...]).astype(BF16)

    def up_task(c):
        up_ref[c % 2, :, 0:TF] = _dot(hf, wup_bf_ref[:, c * TF:(c + 1) * TF])
        up_ref[c % 2, :, TF:2 * TF] = _dot(hf, wup_bf_ref[:, D_FF + c * TF:D_FF + (c + 1) * TF])

    def act_task(c):
        slot = c % 2

        def conv(col0, off):
            parts = []
            for b in range(blocks):
                base = b * MOBA_BLOCK
                prefix = []
                for i in range(tail_groups):
                    r_now = base + MOBA_BLOCK - tail + i * SUBLANES
                    if b:
                        before = up_ref[slot, r_now - MOBA_BLOCK:r_now - MOBA_BLOCK + SUBLANES, col0:col0 + TF]
                    else:
                        before = tail_ref[c, i * SUBLANES:(i + 1) * SUBLANES, col0:col0 + TF]
                    prefix.append(_prev_step_group(
                        before, up_ref[slot, r_now:r_now + SUBLANES, col0:col0 + TF], row))
                ext = jnp.concatenate(prefix + [up_ref[slot, base:base + MOBA_BLOCK, col0:col0 + TF]], axis=0)
                y = cb_ref[:, off:off + TF]
                for j in range(FFN_CONV):
                    y = y + cw_ref[j:j + 1, off:off + TF] * ext[j * SUBLANES:j * SUBLANES + MOBA_BLOCK, :]
                parts.append(y)
            return jnp.concatenate(parts, axis=0)

        act = jax.nn.gelu(conv(0, c * TF)) * conv(TF, D_FF + c * TF)
        act_ref[:, c * TF:(c + 1) * TF] = act.astype(BF16)
        tail_ref[c] = up_ref[slot, tm - tail:tm, :]

    up_task(0)
    for c in range(n_chunks):
        if c + 1 < n_chunks:
            up_task(c + 1)
        act_task(c)

    _store_time_unpermuted(o_ref, stage_ref, x1 + _dot(act_ref[...], wdn_bf_ref[...]), blocks)


def _ffn(x, ylru, yattn, ag, wout, fg, wup, cw, cb, wdn):
    bsz, seq, _ = x.shape
    full = lambda shape: pl.BlockSpec(shape, lambda b, t: (0,) * len(shape), pipeline_mode=pl.Buffered(1))
    tile = lambda w: pl.BlockSpec((None, TM_FFN, w), lambda b, t: (b, t, 0))
    return pl.pallas_call(
        _ffn_kernel,
        grid=(bsz, seq // TM_FFN),
        in_specs=[
            tile(D_MODEL), tile(LRU_WIDTH), tile(ATTN_WIDTH),
            full((1, ATTN_WIDTH)),
            full((D_MODEL, D_MODEL)),
            full((1, D_MODEL)),
            pl.BlockSpec(memory_space=pl.ANY),
            full((FFN_CONV, 2 * D_FF)),
            full((1, 2 * D_FF)),
            full((D_FF, D_MODEL)),
        ],
        out_specs=tile(D_MODEL),
        out_shape=jax.ShapeDtypeStruct((bsz, seq, D_MODEL), F32),
        scratch_shapes=[
            pltpu.VMEM((2, TM_FFN, 2 * TF), F32),
            pltpu.VMEM((D_FF // TF, (FFN_CONV - 1) * SUBLANES, 2 * TF), F32),
            pltpu.VMEM((TM_FFN, D_FF), BF16),
            pltpu.VMEM((D_MODEL // LANES, (TM_FFN // MOBA_BLOCK) * STAGE_BLOCK_ROWS, LANES), F32),
            pltpu.VMEM((D_MODEL, D_MODEL), BF16),
            pltpu.VMEM((D_FF, D_MODEL), BF16),
            pltpu.VMEM((D_MODEL, 2 * D_FF), BF16),
            pltpu.VMEM((2, D_MODEL, W_CHUNK), F32),
            pltpu.SemaphoreType.DMA((2,)),
        ],
        compiler_params=pltpu.CompilerParams(
            dimension_semantics=("arbitrary", "arbitrary"), vmem_limit_bytes=VMEM_LIMIT),
        name="ffn",
    )(x, ylru, yattn, ag, wout, fg, wup, cw, cb, wdn)


def _block_diag_halves(w):
    hd = w.shape[-1]
    w4 = w.reshape(2, LRU_HEADS // 2, hd, hd)
    eye = jnp.eye(LRU_HEADS // 2, dtype=w.dtype)
    return jnp.einsum('ghij,hk->ghikj', w4, eye).reshape(2, LRU_WIDTH // 2, LRU_WIDTH // 2)


def kernel(x, mix_norm_g, w_in, lru_conv_w, lru_conv_b, lru_w_a, lru_b_a, lru_w_i, lru_b_i, lru_lambda,
           q_norm_g, k_norm_g, lru_out_g, attn_out_g, w_out, ffn_norm_g, w_up, ffn_conv_w, ffn_conv_b, w_down):
    depth = w_in.shape[0]
    bsz, seq, _ = x.shape
    assert seq % MOBA_BLOCK == 0 and seq // MOBA_BLOCK > MOBA_TOPK
    row = lambda p: p.reshape(1, -1)
    for layer in range(depth):
        ylru, q, k, v, kmean = _mix_in(
            x, row(mix_norm_g[layer]), w_in[layer],
            lru_conv_w[layer], row(lru_conv_b[layer]),
            _block_diag_halves(lru_w_a[layer]).astype(BF16), row(lru_b_a[layer]),
            _block_diag_halves(lru_w_i[layer]).astype(BF16), row(lru_b_i[layer]),
            row(lru_lambda[layer]),
            row(jnp.tile(q_norm_g[layer], ATTN_HEADS)), row(jnp.tile(k_norm_g[layer], ATTN_HEADS)),
            row(lru_out_g[layer]))
        yattn = _moba(q, k, v, kmean.reshape(bsz, seq // MOBA_BLOCK, ATTN_WIDTH))
        x = _ffn(x, ylru, yattn, row(attn_out_g[layer]), w_out[layer],
                 row(ffn_norm_g[layer]), w_up[layer],
                 ffn_conv_w[layer], row(ffn_conv_b[layer]), w_down[layer])
    return x
```
